```python
import functools
import jax
import jax.numpy as jnp
from jax import lax
import numpy as np

D_MODEL = 1024
BATCH = 8
SEQ = 4096
DEPTH = 2
DEC_BATCH = 128
DEC_SEQ = 4
PAST_LEN = 16384
PAGE_SIZE = 128

N_AB = (DEPTH + 1) // 2
N_C = DEPTH // 2

CONV_W = D_MODEL // 2
CONV_K = 3
N_HEADS = 8
NOPE_DIM = 64
ROPE_DIM = 32
V_DIM = 64
KV_RANK = 256
Q_RANK = 384
ROPE_THETA = 10000.0
ATTN_SCALE = (NOPE_DIM + ROPE_DIM) ** -0.5
Q_BLOCK = 128
MIX_WIDTH_AB = CONV_W + N_HEADS * V_DIM
IN_AB = 3 * CONV_W + Q_RANK + KV_RANK + ROPE_DIM
CHUNK = 128
GMLP_WIDTH = D_MODEL
GMLP_GROUPS = 8
GMLP_GROUP_W = GMLP_WIDTH // GMLP_GROUPS
D_FF = 4 * D_MODEL
PLE_DIM = 256
EPS = 1e-6

kernel_name = 'hybrid_conv_mla_gmlp_decoder_step'


def _rms(x, g):
    xf = x.astype(jnp.float32)
    y = xf * lax.rsqrt(jnp.mean(xf * xf, axis=-1, keepdims=True) + EPS)
    return (y * g.astype(jnp.float32)).astype(x.dtype)


def _split_last(z, sizes):
    parts, off = [], 0
    for s in sizes:
        parts.append(z[..., off:off + s])
        off += s
    return parts


def _rope_tables(pos):
    inv = ROPE_THETA ** (-jnp.arange(0, ROPE_DIM, 2, dtype=jnp.float32) / ROPE_DIM)
    ang = pos.astype(jnp.float32)[:, None] * inv[None, :]
    return jnp.cos(ang), jnp.sin(ang)


def _rope(x, cos, sin):
    half = ROPE_DIM // 2
    shape = (1, cos.shape[0]) + (1,) * (x.ndim - 3) + (half,)
    c = cos.reshape(shape)
    s = sin.reshape(shape)
    xf = x.astype(jnp.float32)
    x1, x2 = xf[..., :half], xf[..., half:]
    return jnp.concatenate([x1 * c - x2 * s, x2 * c + x1 * s], axis=-1).astype(x.dtype)


def _mla_attend_causal(q_lat, q_pe, ckv, kpe):
    b, t, h, r = q_lat.shape
    n_blk = t // Q_BLOCK
    qb = q_lat.reshape(b, n_blk, Q_BLOCK, h, r).transpose(1, 0, 2, 3, 4)
    pb = q_pe.reshape(b, n_blk, Q_BLOCK, h, ROPE_DIM).transpose(1, 0, 2, 3, 4)
    kpos = jnp.arange(t)

    def block(args):
        ql, qp, i = args
        s = jnp.einsum('bqhr,bkr->bhqk', ql, ckv) + jnp.einsum('bqhp,bkp->bhqk', qp, kpe)
        qpos = i * Q_BLOCK + jnp.arange(Q_BLOCK)
        s = jnp.where(kpos[None, :] <= qpos[:, None], s.astype(jnp.float32) * ATTN_SCALE, -jnp.inf)
        pr = jax.nn.softmax(s, axis=-1).astype(ckv.dtype)
        return jnp.einsum('bhqk,bkr->bqhr', pr, ckv)

    o = lax.map(block, (qb, pb, jnp.arange(n_blk)))
    return o.transpose(1, 0, 2, 3, 4).reshape(b, t, h, r)


def _mla_attend_paged(q_lat, q_pe, ckv, kpe, pool_ckv, pool_kpe, page_table):
    b, t, h, r = q_lat.shape
    past_c = pool_ckv[page_table]
    past_p = pool_kpe[page_table]
    n_pages, page = past_c.shape[1], past_c.shape[2]
    n_past = n_pages * page
    s_past = (jnp.einsum('bthr,bnsr->bhtns', q_lat, past_c)
              + jnp.einsum('bthp,bnsp->bhtns', q_pe, past_p)).reshape(b, h, t, n_past)
    s_new = jnp.einsum('bthr,bkr->bhtk', q_lat, ckv) + jnp.einsum('bthp,bkp->bhtk', q_pe, kpe)
    causal = jnp.tril(jnp.ones((t, t), dtype=bool))
    s_new = jnp.where(causal, s_new.astype(jnp.float32) * ATTN_SCALE, -jnp.inf)
    s = jnp.concatenate([s_past.astype(jnp.float32) * ATTN_SCALE, s_new], axis=-1)
    pr = jax.nn.softmax(s, axis=-1).astype(ckv.dtype)
    pr_past = pr[..., :n_past].reshape(b, h, t, n_pages, page)
    return (jnp.einsum('bhtns,bnsr->bthr', pr_past, past_c)
            + jnp.einsum('bhtk,bkr->bthr', pr[..., n_past:], ckv))


def _conv_mla_mixer(a, conv_prev, cos, sin, w_in, conv_w, q_norm, w_uq, kv_norm, w_uk, w_uv, w_out, attend):
    b, t, _ = a.shape
    hc, gc, gb, cq, ckv, kpe = _split_last(a @ w_in, (CONV_W, CONV_W, CONV_W, Q_RANK, KV_RANK, ROPE_DIM))
    up = jnp.concatenate([conv_prev, gc * hc], axis=1)
    conv = conv_w[0] * up[:, 0:t]
    for k in range(1, CONV_K):
        conv = conv + conv_w[k] * up[:, k:k + t]
    y_conv = gb * conv
    conv_state = up[:, t:]
    ckv = _rms(ckv, kv_norm)
    kpe = _rope(kpe, cos, sin)
    q = (_rms(cq, q_norm) @ w_uq).reshape(b, t, N_HEADS, NOPE_DIM + ROPE_DIM)
    q_nope = q[..., :NOPE_DIM]
    q_pe = _rope(q[..., NOPE_DIM:], cos, sin)
    q_lat = jnp.einsum('bthn,rhn->bthr', q_nope, w_uk)
    o_lat = attend(q_lat, q_pe, ckv, kpe)
    y_att = jnp.einsum('bthr,rhv->bthv', o_lat, w_uv).reshape(b, t, N_HEADS * V_DIM)
    y = jnp.concatenate([y_conv, y_att], axis=-1) @ w_out
    return y, conv_state, ckv, kpe


def _chunk_gmlp_mixer(a, w_in, v_norm, w_s, b_s, w_out):
    b, t, _ = a.shape
    z = jax.nn.gelu(a @ w_in, approximate=False)
    u = z[..., :GMLP_WIDTH]
    v = _rms(z[..., GMLP_WIDTH:], v_norm)
    n_chunks = -(-t // CHUNK)
    vp = jnp.pad(v, ((0, 0), (0, n_chunks * CHUNK - t), (0, 0)))
    vp = vp.reshape(b, n_chunks, CHUNK, GMLP_GROUPS, GMLP_GROUP_W)
    ws = jnp.where(jnp.tril(jnp.ones((CHUNK, CHUNK), dtype=bool)), w_s, jnp.zeros_like(w_s))
    s = jnp.einsum('gts,bcsgd->bctgd', ws, vp) + b_s.T[None, None, :, :, None]
    s = s.reshape(b, n_chunks * CHUNK, GMLP_WIDTH)[:, :t]
    return (u * s) @ w_out, v


def _sqrelu_mlp(a, w1, w2):
    hid = jnp.maximum(a @ w1, 0)
    return (hid * hid) @ w2


def _trunk(x, p, pos0, conv_prev, attend_fns, prm):
    b, t, _ = x.shape
    cos, sin = _rope_tables(pos0 + jnp.arange(t, dtype=jnp.int32))
    h = x
    conv_out, ckv_out, kpe_out, v_out = [], [], [], []
    for i in range(DEPTH):
        j = i // 2
        a = _rms(h, prm['norm_mix'][i])
        if i % 2 == 0:
            y, cs, ckv, kpe = _conv_mla_mixer(
                a, conv_prev[j], cos, sin, prm['w_in_ab'][j], prm['conv_w'][j], prm['q_norm'][j],
                prm['w_uq'][j], prm['kv_norm'][j], prm['w_uk'][j], prm['w_uv'][j], prm['w_out_ab'][j],
                attend_fns[j])
            conv_out.append(cs)
            ckv_out.append(ckv)
            kpe_out.append(kpe)
        else:
            y, v = _chunk_gmlp_mixer(a, prm['w_in_c'][j], prm['v_norm'][j], prm['w_s'][j],
                                     prm['b_s'][j], prm['w_out_c'][j])
            v_out.append(v)
        h = h + y
        h = h + _sqrelu_mlp(_rms(h, prm['norm_ffn'][i]), prm['w_ff1'][i], prm['w_ff2'][i])
        gate = jax.nn.sigmoid(_rms(h, prm['norm_ple'][i]) @ prm['w_ple_gate'][i])
        h = h + gate * (p[i] @ prm['w_ple'][i])
    return _rms(h, prm['norm_f']), conv_out, ckv_out, kpe_out, v_out


def setup_inputs(seed: int = 0) -> dict:
    key = jax.random.key(seed)
    k = jax.random.split(key, 29)
    f32 = jnp.float32

    def nrm(kk, shape, scale):
        return jax.random.normal(kk, shape, f32) * scale

    def gain(kk, shape):
        return 1.0 + 0.05 * jax.random.normal(kk, shape, f32)

    n_pages = PAST_LEN // PAGE_SIZE
    n_used = DEC_BATCH * n_pages
    n_pool = n_used + n_used // 4
    page_table = jax.random.permutation(k[5], n_pool)[:n_used].reshape(DEC_BATCH, n_pages).astype(jnp.int32)
    return {
        'x_prompt': nrm(k[0], (BATCH, SEQ, D_MODEL), 1.0),
        'x_sample': nrm(k[1], (DEC_BATCH, DEC_SEQ, D_MODEL), 1.0),
        'cache_ckv': nrm(k[2], (N_AB, n_pool, PAGE_SIZE, KV_RANK), 1.0),
        'cache_kpe': nrm(k[3], (N_AB, n_pool, PAGE_SIZE, ROPE_DIM), 1.0),
        'state_conv': nrm(k[4], (N_AB, DEC_BATCH, CONV_K - 1, CONV_W), 1.0),
        'page_table': page_table,
        'p_prompt': nrm(k[6], (DEPTH, BATCH, SEQ, PLE_DIM), 1.0),
        'p_sample': nrm(k[7], (DEPTH, DEC_BATCH, DEC_SEQ, PLE_DIM), 1.0),
        'norm_mix': gain(k[8], (DEPTH, D_MODEL)),
        'w_in_ab': nrm(k[9], (N_AB, D_MODEL, IN_AB), D_MODEL ** -0.5),
        'conv_w': nrm(k[10], (N_AB, CONV_K, CONV_W), CONV_K ** -0.5),
        'q_norm': gain(k[11], (N_AB, Q_RANK)),
        'w_uq': nrm(k[12], (N_AB, Q_RANK, N_HEADS * (NOPE_DIM + ROPE_DIM)), Q_RANK ** -0.5),
        'kv_norm': gain(k[13], (N_AB, KV_RANK)),
        'w_uk': nrm(k[14], (N_AB, KV_RANK, N_HEADS, NOPE_DIM), KV_RANK ** -0.5),
        'w_uv': nrm(k[15], (N_AB, KV_RANK, N_HEADS, V_DIM), KV_RANK ** -0.5),
        'w_out_ab': nrm(k[16], (N_AB, MIX_WIDTH_AB, D_MODEL), MIX_WIDTH_AB ** -0.5),
        'w_in_c': nrm(k[17], (N_C, D_MODEL, 2 * GMLP_WIDTH), D_MODEL ** -0.5),
        'v_norm': gain(k[18], (N_C, GMLP_WIDTH)),
        'w_s': nrm(k[19], (N_C, GMLP_GROUPS, CHUNK, CHUNK), CHUNK ** -0.5),
        'b_s': 1.0 + 0.02 * jax.random.normal(k[20], (N_C, GMLP_GROUPS, CHUNK), f32),
        'w_out_c': nrm(k[21], (N_C, GMLP_WIDTH, D_MODEL), GMLP_WIDTH ** -0.5),
        'norm_ffn': gain(k[22], (DEPTH, D_MODEL)),
        'w_ff1': nrm(k[23], (DEPTH, D_MODEL, D_FF), D_MODEL ** -0.5),
        'w_ff2': nrm(k[24], (DEPTH, D_FF, D_MODEL), D_FF ** -0.5),
        'norm_ple': gain(k[25], (DEPTH, D_MODEL)),
        'w_ple_gate': nrm(k[26], (DEPTH, D_MODEL, D_MODEL), D_MODEL ** -0.5),
        'w_ple': nrm(k[27], (DEPTH, PLE_DIM, D_MODEL), PLE_DIM ** -0.5),
        'norm_f': gain(k[28], (D_MODEL,)),
    }


def reference(x_prompt, x_sample, cache_ckv, cache_kpe, state_conv, page_table, p_prompt, p_sample,
              norm_mix, w_in_ab, conv_w, q_norm, w_uq, kv_norm, w_uk, w_uv, w_out_ab,
              w_in_c, v_norm, w_s, b_s, w_out_c, norm_ffn, w_ff1, w_ff2, norm_ple, w_ple_gate, w_ple, norm_f):
    prm = dict(norm_mix=norm_mix, w_in_ab=w_in_ab, conv_w=conv_w, q_norm=q_norm, w_uq=w_uq,
               kv_norm=kv_norm, w_uk=w_uk, w_uv=w_uv, w_out_ab=w_out_ab, w_in_c=w_in_c,
               v_norm=v_norm, w_s=w_s, b_s=b_s, w_out_c=w_out_c, norm_ffn=norm_ffn, w_ff1=w_ff1,
               w_ff2=w_ff2, norm_ple=norm_ple, w_ple_gate=w_ple_gate, w_ple=w_ple, norm_f=norm_f)
    conv_zero = jnp.zeros((x_prompt.shape[0], CONV_K - 1, CONV_W), x_prompt.dtype)
    y_prompt, conv_p, ckv_p, kpe_p, _ = _trunk(
        x_prompt, p_prompt, 0, [conv_zero] * N_AB, [_mla_attend_causal] * N_AB, prm)
    attend_s = [functools.partial(_mla_attend_paged, pool_ckv=cache_ckv[j], pool_kpe=cache_kpe[j],
                                  page_table=page_table) for j in range(N_AB)]
    y_sample, conv_s, ckv_s, kpe_s, v_s = _trunk(
        x_sample, p_sample, PAST_LEN, [state_conv[j] for j in range(N_AB)], attend_s, prm)
    conv_prompt = jnp.stack(conv_p)
    conv_sample = jnp.stack(conv_s)
    ckv_prompt = jnp.stack(ckv_p)
    kpe_prompt = jnp.stack(kpe_p)
    ckv_sample = jnp.stack(ckv_s)
    kpe_sample = jnp.stack(kpe_s)
    v_sample = jnp.stack(v_s)
    return (y_prompt, y_sample, conv_prompt, conv_sample, ckv_prompt, kpe_prompt, ckv_sample, kpe_sample, v_sample)
```

```python
import functools
import math

import jax
import jax.numpy as jnp
from jax import lax
from jax.experimental import pallas as pl
from jax.experimental.pallas import tpu as pltpu

F32 = jnp.float32
BF16 = jnp.bfloat16

EPS = 1e-6
ROPE_THETA = 10000.0
LANES = 128
SUBLANES = 8
VMEM_LIMIT_BYTES = 56 * 1024 * 1024
PAGES_PER_CHUNK = 16
MASK_VALUE = -1e30


def _rms(x, g):
    ms = jnp.mean(x * x, axis=-1, keepdims=True)
    return x * lax.rsqrt(ms + EPS) * g


def _const_spec(shape):
    nd = len(shape)
    return pl.BlockSpec(shape, lambda *_: (0,) * nd, pipeline_mode=pl.Buffered(1))


def _params(semantics):
    return pltpu.CompilerParams(dimension_semantics=semantics,
                                vmem_limit_bytes=VMEM_LIMIT_BYTES)


def _ab_project(x_ref, tab_ref, gmix_ref, win_ref, gq_ref, wq_ref, gkv_ref, cw_ref, rope):
    cw = cw_ref.shape[1]
    q_rank = gq_ref.shape[1]
    kv_rank = gkv_ref.shape[1]
    tab = tab_ref[...]
    a = _rms(x_ref[...], gmix_ref[...]).astype(BF16)
    z = jnp.dot(a, win_ref[...], preferred_element_type=F32)
    o_q = 3 * cw
    o_kv = o_q + q_rank
    o_pe = o_kv + kv_rank
    up = z[:, cw:2 * cw] * z[:, 0:cw]
    gb = z[:, 2 * cw:o_q]
    ckv_n = _rms(z[:, o_kv:o_pe], gkv_ref[...])
    prod = z[:, o_pe:o_pe + LANES] * tab[:, 0:LANES]
    kr = prod + pltpu.roll(prod, LANES - rope, axis=1)
    return up, gb, ckv_n, kr, _rotated_q(z[:, o_q:o_kv], tab, gq_ref[...], wq_ref[...])


def _rotated_q(cq, tab, g_q, w_q):
    qn = _rms(cq, g_q).astype(BF16)
    q2 = jnp.dot(qn, w_q, preferred_element_type=F32)
    half = q2.shape[1] // 2
    n_heads = half // LANES
    cpat = jnp.concatenate([tab[:, LANES:2 * LANES]] * n_heads, axis=1)
    spat = jnp.concatenate([tab[:, 2 * LANES:3 * LANES]] * n_heads, axis=1)
    return q2[:, :half] * cpat + q2[:, half:] * spat


def _ab_prompt_kernel(x_ref, tab_ref, gmix_ref, win_ref, gq_ref, wq_ref, gkv_ref, cw_ref,
                      wk_ref, ek_ref, wv_ref,
                      yconv_ref, ckv_ref, kpe_ref, q_ref, k_ref, v_ref, cstate_ref,
                      carry_ref, *, tiles_per_seq):
    i = pl.program_id(0)

    @pl.when(i % tiles_per_seq == 0)
    def _():
        carry_ref[...] = jnp.zeros_like(carry_ref)

    up, gb, ckv_n, kr, q_r = _ab_project(
        x_ref, tab_ref, gmix_ref, win_ref, gq_ref, wq_ref, gkv_ref, cw_ref, kpe_ref.shape[1])
    tm = up.shape[0]
    c0 = carry_ref[SUBLANES - 2:SUBLANES - 1, :]
    c1 = carry_ref[SUBLANES - 1:SUBLANES, :]
    row = lax.broadcasted_iota(jnp.int32, (tm, 1), 0)
    um1 = jnp.where(row == 0, c1, pltpu.roll(up, 1, axis=0))
    um2 = jnp.where(row == 0, c0, jnp.where(row == 1, c1, pltpu.roll(up, 2, axis=0)))
    cw = cw_ref[...]
    conv = cw[0:1, :] * um2 + cw[1:2, :] * um1 + cw[2:3, :] * up
    yconv_ref[...] = (gb * conv).astype(yconv_ref.dtype)
    tail = up[tm - SUBLANES:, :]
    carry_ref[...] = tail
    cstate_ref[0] = tail

    ckv_ref[...] = ckv_n
    kpe_ref[...] = kr[:, 0:kpe_ref.shape[1]]
    q_ref[...] = q_r.astype(q_ref.dtype)
    ckv_b = ckv_n.astype(BF16)
    k_cat = (jnp.dot(ckv_b, wk_ref[...], preferred_element_type=F32)
             + jnp.dot(kr.astype(BF16), ek_ref[...], preferred_element_type=F32))
    k_ref[...] = k_cat.astype(k_ref.dtype)
    v_ref[...] = jnp.dot(ckv_b, wv_ref[...], preferred_element_type=F32).astype(v_ref.dtype)


def _ab_sample_kernel(x_ref, tab_ref, gmix_ref, win_ref, gq_ref, wq_ref, gkv_ref, cw_ref,
                      wukt_ref, state_ref,
                      yconv_ref, ckv_ref, kpe_ref, q_ref, qlat_ref, up_ref,
                      prev1_ref, prev2_ref):
    t = pl.program_id(0)

    @pl.when(t == 0)
    def _():
        prev2_ref[...] = state_ref[0]
        prev1_ref[...] = state_ref[1]

    up, gb, ckv_n, kr, q_r = _ab_project(
        x_ref, tab_ref, gmix_ref, win_ref, gq_ref, wq_ref, gkv_ref, cw_ref, kpe_ref.shape[1])
    cw = cw_ref[...]
    prev1 = prev1_ref[...]
    conv = cw[0:1, :] * prev2_ref[...] + cw[1:2, :] * prev1 + cw[2:3, :] * up
    yconv_ref[...] = (gb * conv).astype(yconv_ref.dtype)
    up_ref[...] = up
    prev2_ref[...] = prev1
    prev1_ref[...] = up

    ckv_ref[...] = ckv_n
    kpe_ref[...] = kr[:, 0:kpe_ref.shape[1]]
    q_b = q_r.astype(BF16)
    q_ref[...] = q_b
    for h in range(qlat_ref.shape[0]):
        qlat_ref[h] = jnp.dot(q_b[:, h * LANES:(h + 1) * LANES], wukt_ref[h],
                              preferred_element_type=F32).astype(qlat_ref.dtype)


def _flash_kernel(q_ref, k_ref, v_ref, o_ref, m_ref, l_ref, acc_ref, *, scale, n_heads, v_dim):
    i = pl.program_id(1)
    j = pl.program_id(2)

    @pl.when(j == 0)
    def _():
        m_ref[...] = jnp.full_like(m_ref, MASK_VALUE)
        l_ref[...] = jnp.zeros_like(l_ref)
        acc_ref[...] = jnp.zeros_like(acc_ref)

    def step(masked):
        tq = q_ref.shape[1]
        tk = k_ref.shape[1]
        if masked:
            rows = lax.broadcasted_iota(jnp.int32, (tq, tk), 0)
            cols = lax.broadcasted_iota(jnp.int32, (tq, tk), 1)
            keep = cols <= rows
        for h in range(n_heads):
            q = q_ref[0, :, h * LANES:(h + 1) * LANES]
            k = k_ref[0, :, h * LANES:(h + 1) * LANES]
            s = lax.dot_general(q, k, (((1,), (1,)), ((), ())),
                                preferred_element_type=F32) * scale
            if masked:
                s = jnp.where(keep, s, MASK_VALUE)
            m_old = m_ref[h]
            m_new = jnp.maximum(m_old, jnp.max(s, axis=1, keepdims=True))
            alpha = jnp.exp(m_old - m_new)
            p = jnp.exp(s - m_new)
            l_ref[h] = alpha * l_ref[h] + jnp.sum(p, axis=1, keepdims=True)
            v = v_ref[0, :, h * v_dim:(h + 1) * v_dim]
            acc_ref[h] = alpha * acc_ref[h] + jnp.dot(p.astype(BF16), v,
                                                      preferred_element_type=F32)
            m_ref[h] = m_new

    @pl.when(j < i)
    def _():
        step(False)

    @pl.when(j == i)
    def _():
        step(True)
        for h in range(n_heads):
            o_ref[0, :, h * v_dim:(h + 1) * v_dim] = (acc_ref[h] / l_ref[h]).astype(o_ref.dtype)


def _paged_kernel(pt_ref, ql_ref, qp_ref, kn_ref, pn_ref, wuv_ref, pool_c, pool_p,
                  o_ref, cbuf, pbuf, sem_c, sem_p, *, scale, n_pages, n_heads, v_dim):
    b = pl.program_id(0)
    n_b = pl.num_programs(0)
    cp = cbuf.shape[1]
    page = cbuf.shape[2]
    n_chunks = n_pages // cp

    def chunk_copies(bb, c, slot):
        copies = []
        for k in range(cp):
            pg = pt_ref[bb * n_pages + c * cp + k]
            copies.append(pltpu.make_async_copy(pool_c.at[pg], cbuf.at[slot, k], sem_c.at[slot]))
            copies.append(pltpu.make_async_copy(pool_p.at[pg], pbuf.at[slot, k], sem_p.at[slot]))
        return copies

    def start_chunk(bb, c, slot):
        for cpy in chunk_copies(bb, c, slot):
            cpy.start()

    def wait_chunk(slot):
        for k in range(cp):
            pltpu.make_async_copy(pool_c.at[0], cbuf.at[slot, k], sem_c.at[slot]).wait()
            pltpu.make_async_copy(pool_p.at[0], pbuf.at[slot, k], sem_p.at[slot]).wait()

    @pl.when(b == 0)
    def _():
        start_chunk(0, 0, 0)

    ql = ql_ref[0]
    qp = qp_ref[0]
    rows = ql.shape[0]
    nt = (((1,), (1,)), ((), ()))

    kn = kn_ref[0].astype(BF16)
    pn = pn_ref[0].astype(BF16)
    s0 = (lax.dot_general(ql, kn, nt, preferred_element_type=F32)
          + lax.dot_general(qp, pn, nt, preferred_element_type=F32)) * scale
    r_id = lax.broadcasted_iota(jnp.int32, s0.shape, 0)
    c_id = lax.broadcasted_iota(jnp.int32, s0.shape, 1)
    s0 = jnp.where(c_id <= r_id // n_heads, s0, MASK_VALUE)
    m0 = jnp.max(s0, axis=1, keepdims=True)
    p0 = jnp.exp(s0 - m0)
    l0 = jnp.sum(p0, axis=1, keepdims=True)
    acc0 = jnp.dot(p0.astype(BF16), kn, preferred_element_type=F32)

    def body(c, carry):
        m_old, l_old, acc = carry
        g = b * n_chunks + c
        slot = g % 2

        @pl.when(c + 1 < n_chunks)
        def _():
            start_chunk(b, c + 1, 1 - slot)

        @pl.when(jnp.logical_and(c + 1 == n_chunks, b + 1 < n_b))
        def _():
            start_chunk(b + 1, 0, 1 - slot)

        wait_chunk(slot)
        kc = cbuf[slot].reshape(cp * page, cbuf.shape[3]).astype(BF16)
        kp = pbuf[slot].reshape(cp * page, pbuf.shape[3]).astype(BF16)
        s = (lax.dot_general(ql, kc, nt, preferred_element_type=F32)
             + lax.dot_general(qp, kp, nt, preferred_element_type=F32)) * scale
        m_new = jnp.maximum(m_old, jnp.max(s, axis=1, keepdims=True))
        alpha = jnp.exp(m_old - m_new)
        p = jnp.exp(s - m_new)
        l_new = alpha * l_old + jnp.sum(p, axis=1, keepdims=True)
        acc = alpha * acc + jnp.dot(p.astype(BF16), kc, preferred_element_type=F32)
        return m_new, l_new, acc

    _, l_fin, acc = lax.fori_loop(0, n_chunks, body, (m0, l0, acc0))
    o_lat = (acc / l_fin).astype(BF16)
    full = jnp.dot(o_lat, wuv_ref[...], preferred_element_type=F32)
    r_id = lax.broadcasted_iota(jnp.int32, full.shape, 0)
    c_id = lax.broadcasted_iota(jnp.int32, full.shape, 1)
    full = jnp.where(c_id // v_dim == r_id % n_heads, full, 0.0)
    o_ref[0] = jnp.sum(full.reshape(rows // n_heads, n_heads, full.shape[1]), axis=1)


def _gelu(x):
    return 0.5 * x * (1.0 + lax.erf(x * (1.0 / math.sqrt(2.0))))


def _gmlp_uv(x, g_mix, w_in, g_v):
    a = _rms(x, g_mix).astype(BF16)
    z = _gelu(jnp.dot(a, w_in, preferred_element_type=F32))
    width = z.shape[1] // 2
    return z[:, :width], _rms(z[:, width:], g_v)


def _gmlp_prompt_kernel(x_ref, gmix_ref, win_ref, gv_ref, ws_ref, bias_ref, m_ref, *, chunk):
    u, v = _gmlp_uv(x_ref[...], gmix_ref[...], win_ref[...], gv_ref[...])
    vb = v.astype(BF16)
    n_groups = ws_ref.shape[0]
    r_id = lax.broadcasted_iota(jnp.int32, (chunk, chunk), 0)
    c_id = lax.broadcasted_iota(jnp.int32, (chunk, chunk), 1)
    tril = c_id <= r_id
    bias = bias_ref[...]
    for g in range(n_groups):
        wg = jnp.where(tril, ws_ref[g], 0.0).astype(BF16)
        gs = slice(g * LANES, (g + 1) * LANES)
        for c in range(x_ref.shape[0] // chunk):
            rs = slice(c * chunk, (c + 1) * chunk)
            s = jnp.dot(wg, vb[rs, gs], preferred_element_type=F32) + bias[:, gs]
            m_ref[rs, gs] = (u[rs, gs] * s).astype(m_ref.dtype)


def _gmlp_sample_kernel(x_ref, gmix_ref, win_ref, gv_ref, ws_ref, bias_ref, m_ref, v_ref, *, n_t):
    u, v = _gmlp_uv(x_ref[...], gmix_ref[...], win_ref[...], gv_ref[...])
    v_ref[...] = v
    db = x_ref.shape[0] // n_t
    vb = v.astype(BF16).astype(F32)
    for t in range(n_t):
        s = bias_ref[t:t + 1, :]
        for k in range(t + 1):
            w = ws_ref[t, k:k + 1, :].astype(BF16).astype(F32)
            s = s + w * vb[k * db:(k + 1) * db, :]
        m_ref[t * db:(t + 1) * db, :] = (u[t * db:(t + 1) * db, :] * s).astype(m_ref.dtype)


def _post_kernel(*refs, n_mix, ff_chunk, final):
    h_ref = refs[0]
    mix_refs = refs[1:1 + n_mix]
    wout_refs = refs[1 + n_mix:1 + 2 * n_mix]
    (p_ref, gffn_ref, w1_ref, w2_ref, gple_ref, wg_ref, wple_ref, gf_ref, out_ref) = refs[1 + 2 * n_mix:]
    h = h_ref[...]
    for m_ref, w_ref in zip(mix_refs, wout_refs):
        h = h + jnp.dot(m_ref[...], w_ref[...], preferred_element_type=F32)
    a = _rms(h, gffn_ref[...]).astype(BF16)
    d_ff = w1_ref.shape[1]
    y = None
    for c in range(d_ff // ff_chunk):
        cs = slice(c * ff_chunk, (c + 1) * ff_chunk)
        hid = jnp.maximum(jnp.dot(a, w1_ref[:, cs], preferred_element_type=F32), 0.0)
        part = jnp.dot((hid * hid).astype(BF16), w2_ref[cs, :], preferred_element_type=F32)
        y = part if y is None else y + part
    h = h + y
    a = _rms(h, gple_ref[...]).astype(BF16)
    gate = jax.nn.sigmoid(jnp.dot(a, wg_ref[...], preferred_element_type=F32))
    pe = jnp.dot(p_ref[...].astype(BF16), wple_ref[...], preferred_element_type=F32)
    h = h + gate * pe
    if final:
        h = _rms(h, gf_ref[...])
    out_ref[...] = h


def _row_spec(tm, width):
    return pl.BlockSpec((tm, width), lambda i: (i, 0))


def _ab_common_specs(tm, d, tab_map, w):
    return [
        _row_spec(tm, d),
        pl.BlockSpec((tm, 3 * LANES), tab_map),
        _const_spec((1, d)),
        _const_spec(w['w_in'].shape),
        _const_spec(w['g_q'].shape),
        _const_spec(w['w_q'].shape),
        _const_spec(w['g_kv'].shape),
        _const_spec(w['conv_w'].shape),
    ]


def _ab_prompt(h, tab, w, batch, seq, tm):
    n, d = h.shape
    tiles_per_seq = seq // tm
    conv_w = w['conv_w'].shape[1]
    kv_rank = w['g_kv'].shape[1]
    rope = w['rope_dim']
    qw = w['w_q'].shape[1] // 2
    vw = w['w_v'].shape[1]
    in_specs = _ab_common_specs(tm, d, lambda i: (i % tiles_per_seq, 0), w) + [
        _const_spec(w['w_k'].shape), _const_spec(w['e_k'].shape), _const_spec(w['w_v'].shape)]
    out_shape = (
        jax.ShapeDtypeStruct((n, conv_w), BF16),
        jax.ShapeDtypeStruct((n, kv_rank), F32),
        jax.ShapeDtypeStruct((n, rope), F32),
        jax.ShapeDtypeStruct((n, qw), BF16),
        jax.ShapeDtypeStruct((n, qw), BF16),
        jax.ShapeDtypeStruct((n, vw), BF16),
        jax.ShapeDtypeStruct((batch, SUBLANES, conv_w), F32),
    )
    out_specs = (
        _row_spec(tm, conv_w), _row_spec(tm, kv_rank), _row_spec(tm, rope),
        _row_spec(tm, qw), _row_spec(tm, qw), _row_spec(tm, vw),
        pl.BlockSpec((1, SUBLANES, conv_w), lambda i: (i // tiles_per_seq, 0, 0)),
    )
    return pl.pallas_call(
        functools.partial(_ab_prompt_kernel, tiles_per_seq=tiles_per_seq),
        grid=(n // tm,), in_specs=in_specs, out_specs=out_specs, out_shape=out_shape,
        scratch_shapes=[pltpu.VMEM((SUBLANES, conv_w), F32)],
        compiler_params=_params(("arbitrary",)), name="ab_prompt",
    )(h, tab, w['g_mix'], w['w_in'], w['g_q'], w['w_q'], w['g_kv'], w['conv_w'],
      w['w_k'], w['e_k'], w['w_v'])


def _ab_sample(h, tab, w, state, n_t):
    n, d = h.shape
    db = n // n_t
    conv_w = w['conv_w'].shape[1]
    kv_rank = w['g_kv'].shape[1]
    rope = w['rope_dim']
    qw = w['w_q'].shape[1] // 2
    n_heads = w['w_ukt'].shape[0]
    in_specs = _ab_common_specs(db, d, lambda i: (i, 0), w) + [
        _const_spec(w['w_ukt'].shape), _const_spec(state.shape)]
    out_shape = (
        jax.ShapeDtypeStruct((n, conv_w), BF16),
        jax.ShapeDtypeStruct((n, kv_rank), F32),
        jax.ShapeDtypeStruct((n, rope), F32),
        jax.ShapeDtypeStruct((n, qw), BF16),
        jax.ShapeDtypeStruct((n_heads, n, kv_rank), BF16),
        jax.ShapeDtypeStruct((n, conv_w), F32),
    )
    out_specs = (
        _row_spec(db, conv_w), _row_spec(db, kv_rank), _row_spec(db, rope), _row_spec(db, qw),
        pl.BlockSpec((n_heads, db, kv_rank), lambda i: (0, i, 0)),
        _row_spec(db, conv_w),
    )
    return pl.pallas_call(
        _ab_sample_kernel,
        grid=(n_t,), in_specs=in_specs, out_specs=out_specs, out_shape=out_shape,
        scratch_shapes=[pltpu.VMEM((db, conv_w), F32), pltpu.VMEM((db, conv_w), F32)],
        compiler_params=_params(("arbitrary",)), name="ab_sample",
    )(h, tab, w['g_mix'], w['w_in'], w['g_q'], w['w_q'], w['g_kv'], w['conv_w'],
      w['w_ukt'], state)


def _flash(q, k, v, scale, n_heads, tq):
    batch, seq, qw = q.shape
    vw = v.shape[2]
    v_dim = vw // n_heads
    nq = seq // tq
    return pl.pallas_call(
        functools.partial(_flash_kernel, scale=scale, n_heads=n_heads, v_dim=v_dim),
        grid=(batch, nq, nq),
        in_specs=[
            pl.BlockSpec((1, tq, qw), lambda b, i, j: (b, i, 0)),
            pl.BlockSpec((1, tq, qw), lambda b, i, j: (b, jnp.minimum(i, j), 0)),
            pl.BlockSpec((1, tq, vw), lambda b, i, j: (b, jnp.minimum(i, j), 0)),
        ],
        out_specs=pl.BlockSpec((1, tq, vw), lambda b, i, j: (b, i, 0)),
        out_shape=jax.ShapeDtypeStruct((batch, seq, vw), BF16),
        scratch_shapes=[pltpu.VMEM((n_heads, tq, 1), F32), pltpu.VMEM((n_heads, tq, 1), F32),
                        pltpu.VMEM((n_heads, tq, v_dim), F32)],
        compiler_params=_params(("arbitrary", "arbitrary", "arbitrary")), name="flash_prompt",
    )(q, k, v)


def _paged(page_table, ql, qp, kn, pn, w_uv, pool_c, pool_p, scale, n_heads):
    db, rows, kv_rank = ql.shape
    rope = qp.shape[2]
    n_pages = page_table.shape[1]
    page = pool_c.shape[1]
    vw = w_uv.shape[1]
    n_t = rows // n_heads
    cp = min(PAGES_PER_CHUNK, n_pages)
    assert n_pages % cp == 0
    grid_spec = pltpu.PrefetchScalarGridSpec(
        num_scalar_prefetch=1,
        grid=(db,),
        in_specs=[
            pl.BlockSpec((1, rows, kv_rank), lambda b, pt: (b, 0, 0)),
            pl.BlockSpec((1, rows, rope), lambda b, pt: (b, 0, 0)),
            pl.BlockSpec((1, SUBLANES, kv_rank), lambda b, pt: (b, 0, 0)),
            pl.BlockSpec((1, SUBLANES, rope), lambda b, pt: (b, 0, 0)),
            pl.BlockSpec(w_uv.shape, lambda b, pt: (0, 0)),
            pl.BlockSpec(memory_space=pl.ANY),
            pl.BlockSpec(memory_space=pl.ANY),
        ],
        out_specs=pl.BlockSpec((1, n_t, vw), lambda b, pt: (b, 0, 0)),
        scratch_shapes=[
            pltpu.VMEM((2, cp, page, kv_rank), F32),
            pltpu.VMEM((2, cp, page, rope), F32),
            pltpu.SemaphoreType.DMA((2,)),
            pltpu.SemaphoreType.DMA((2,)),
        ],
    )
    return pl.pallas_call(
        functools.partial(_paged_kernel, scale=scale, n_pages=n_pages, n_heads=n_heads,
                          v_dim=vw // n_heads),
        grid_spec=grid_spec,
        out_shape=jax.ShapeDtypeStruct((db, n_t, vw), F32),
        compiler_params=_params(("arbitrary",)), name="paged_attn",
    )(page_table.reshape(-1), ql, qp, kn, pn, w_uv, pool_c, pool_p)


def _gmlp_prompt(h, w, tm, chunk):
    n, d = h.shape
    width = w['w_in_c'].shape[1] // 2
    return pl.pallas_call(
        functools.partial(_gmlp_prompt_kernel, chunk=chunk),
        grid=(n // tm,),
        in_specs=[_row_spec(tm, d), _const_spec((1, d)), _const_spec(w['w_in_c'].shape),
                  _const_spec((1, width)), _const_spec(w['w_s'].shape),
                  _const_spec(w['bias_full'].shape)],
        out_specs=_row_spec(tm, width),
        out_shape=jax.ShapeDtypeStruct((n, width), BF16),
        compiler_params=_params(("arbitrary",)), name="gmlp_prompt",
    )(h, w['g_mix'], w['w_in_c'], w['g_v'], w['w_s'], w['bias_full'])


def _gmlp_sample(h, w, n_t):
    n, d = h.shape
    width = w['w_in_c'].shape[1] // 2
    return pl.pallas_call(
        functools.partial(_gmlp_sample_kernel, n_t=n_t),
        grid=(1,),
        in_specs=[_const_spec((n, d)), _const_spec((1, d)), _const_spec(w['w_in_c'].shape),
                  _const_spec((1, width)), _const_spec(w['ws_small'].shape),
                  _const_spec(w['bias_small'].shape)],
        out_specs=(_const_spec((n, width)), _const_spec((n, width))),
        out_shape=(jax.ShapeDtypeStruct((n, width), BF16), jax.ShapeDtypeStruct((n, width), F32)),
        compiler_params=_params(("arbitrary",)), name="gmlp_sample",
    )(h, w['g_mix'], w['w_in_c'], w['g_v'], w['ws_small'], w['bias_small'])


def _post(h, mixes, w_outs, p, w, tm, final, name):
    n, d = h.shape
    n_mix = len(mixes)
    ff_chunk = min(1024, w['w1'].shape[1])
    in_specs = ([_row_spec(tm, d)] + [_row_spec(tm, m.shape[1]) for m in mixes]
                + [_const_spec(wo.shape) for wo in w_outs]
                + [_row_spec(tm, p.shape[1]), _const_spec((1, d)), _const_spec(w['w1'].shape),
                   _const_spec(w['w2'].shape), _const_spec((1, d)), _const_spec(w['w_gate'].shape),
                   _const_spec(w['w_ple'].shape), _const_spec((1, d))])
    return pl.pallas_call(
        functools.partial(_post_kernel, n_mix=n_mix, ff_chunk=ff_chunk, final=final),
        grid=(n // tm,), in_specs=in_specs, out_specs=_row_spec(tm, d),
        out_shape=jax.ShapeDtypeStruct((n, d), F32),
        compiler_params=_params(("arbitrary",)), name=name,
    )(h, *mixes, *w_outs, p, w['g_ffn'], w['w1'], w['w2'], w['g_ple'], w['w_gate'], w['w_ple'],
      w['g_f'])


def _rope_table(pos, rope_dim):
    half = rope_dim // 2
    inv = ROPE_THETA ** (-jnp.arange(0, rope_dim, 2, dtype=F32) / rope_dim)
    ang = pos.astype(F32)[:, None] * inv[None, :]
    c, s = jnp.cos(ang), jnp.sin(ang)
    t = pos.shape[0]
    cc = jnp.concatenate([c, c], axis=1)
    ss = jnp.concatenate([-s, s], axis=1)
    zeros = lambda w_: jnp.zeros((t, w_), F32)
    nope = LANES - 2 * rope_dim
    key_tab = jnp.concatenate([cc, ss, zeros(LANES - 2 * rope_dim)], axis=1)
    q_cos = jnp.concatenate([jnp.ones((t, nope), F32), cc, zeros(LANES - nope - rope_dim)], axis=1)
    q_sin = jnp.concatenate([zeros(nope), ss, zeros(LANES - nope - rope_dim)], axis=1)
    del half
    return jnp.concatenate([key_tab, q_cos, q_sin], axis=1)


def _prep_layer_ab(j, norm_mix_i, w_in_ab, conv_w, q_norm, w_uq, kv_norm, w_uk, w_uv):
    d = w_in_ab.shape[1]
    kv_rank, n_heads, nope = w_uk.shape[1:]
    rope = w_uq.shape[2] // n_heads - nope
    v_dim = w_uv.shape[3]
    half = rope // 2
    in_ab = w_in_ab.shape[2]
    w_in = w_in_ab[j]
    kpe0 = in_ab - rope
    pad = (-(in_ab + rope)) % (2 * LANES)
    w_in_p = jnp.concatenate(
        [w_in, w_in[:, kpe0 + half:], w_in[:, kpe0:kpe0 + half], jnp.zeros((d, pad), F32)], axis=1)
    q_rank = w_uq.shape[1]
    uq = w_uq[j].reshape(q_rank, n_heads, nope + rope)
    zpad = jnp.zeros((q_rank, n_heads, LANES - nope - rope), F32)
    wq_a = jnp.concatenate([uq, zpad], axis=2)
    wq_b = jnp.concatenate([jnp.zeros((q_rank, n_heads, nope), F32), uq[:, :, nope + half:],
                            uq[:, :, nope:nope + half], zpad], axis=2)
    w_q = jnp.concatenate([wq_a.reshape(q_rank, -1), wq_b.reshape(q_rank, -1)], axis=1)
    uk = w_uk[j]
    w_k = jnp.concatenate([uk, jnp.zeros((kv_rank, n_heads, LANES - nope), F32)], axis=2)
    e_k = jnp.zeros((LANES, n_heads, LANES), F32)
    idx = jnp.arange(rope)
    e_k = e_k.at[idx, :, nope + idx].set(1.0)
    w_ukt = jnp.concatenate([uk.transpose(1, 2, 0),
                             jnp.zeros((n_heads, LANES - nope, kv_rank), F32)], axis=1)
    return dict(
        g_mix=norm_mix_i.reshape(1, -1), w_in=w_in_p.astype(BF16), g_q=q_norm[j].reshape(1, -1),
        w_q=w_q.astype(BF16), g_kv=kv_norm[j].reshape(1, -1), conv_w=conv_w[j],
        w_k=w_k.reshape(kv_rank, -1).astype(BF16), e_k=e_k.reshape(LANES, -1).astype(BF16),
        w_v=w_uv[j].reshape(kv_rank, n_heads * v_dim).astype(BF16), w_ukt=w_ukt.astype(BF16),
        rope_dim=rope, n_heads=n_heads, nope=nope)


def _prep_layer_c(j, norm_mix_i, w_in_c, v_norm, w_s, b_s, n_t):
    n_groups, chunk = w_s.shape[1], w_s.shape[2]
    width = w_in_c.shape[2] // 2
    gw = width // n_groups
    bias_full = jnp.repeat(b_s[j].T, gw, axis=1)
    ws_small = jnp.repeat(w_s[j][:, :n_t, :n_t].transpose(1, 2, 0), gw, axis=2)
    return dict(g_mix=norm_mix_i.reshape(1, -1), w_in_c=w_in_c[j].astype(BF16),
                g_v=v_norm[j].reshape(1, -1), w_s=w_s[j], bias_full=bias_full,
                ws_small=ws_small, bias_small=bias_full[:n_t], chunk=chunk)


def _prep_post(i, norm_ffn, w_ff1, w_ff2, norm_ple, w_ple_gate, w_ple, norm_f):
    return dict(g_ffn=norm_ffn[i].reshape(1, -1), w1=w_ff1[i].astype(BF16), w2=w_ff2[i].astype(BF16),
                g_ple=norm_ple[i].reshape(1, -1), w_gate=w_ple_gate[i].astype(BF16),
                w_ple=w_ple[i].astype(BF16), g_f=norm_f.reshape(1, -1))


def _token_tile(n, cap):
    tm = min(cap, n)
    while n % tm:
        tm //= 2
    return tm


def kernel(x_prompt, x_sample, cache_ckv, cache_kpe, state_conv, page_table, p_prompt, p_sample, norm_mix, w_in_ab, conv_w, q_norm, w_uq, kv_norm, w_uk, w_uv, w_out_ab, w_in_c, v_norm, w_s, b_s, w_out_c, norm_ffn, w_ff1, w_ff2, norm_ple, w_ple_gate, w_ple, norm_f):
    depth = norm_mix.shape[0]
    batch, seq, d = x_prompt.shape
    db, n_t, _ = x_sample.shape
    n_pages = page_table.shape[1]
    page = cache_ckv.shape[2]
    past_len = n_pages * page
    n_heads, nope = w_uk.shape[2], w_uk.shape[3]
    rope = cache_kpe.shape[3]
    conv_width = conv_w.shape[2]
    scale = float(nope + rope) ** -0.5

    tab_p = _rope_table(jnp.arange(seq, dtype=jnp.int32), rope)
    tab_s = jnp.repeat(_rope_table(past_len + jnp.arange(n_t, dtype=jnp.int32), rope), db, axis=0)

    n_p = batch * seq
    n_s = db * n_t
    tm_p = _token_tile(seq, 512)
    tq = _token_tile(seq, 512)
    hp = x_prompt.reshape(n_p, d)
    hs = x_sample.transpose(1, 0, 2).reshape(n_s, d)
    pp = p_prompt.reshape(depth, n_p, -1)
    ps = p_sample.transpose(0, 2, 1, 3).reshape(depth, n_s, -1)

    def unmajor(a):
        return a.reshape(n_t, db, a.shape[-1]).transpose(1, 0, 2)

    conv_p, conv_s, ckv_p, kpe_p, ckv_s, kpe_s, v_s = [], [], [], [], [], [], []
    for i in range(depth):
        j = i // 2
        wpost = _prep_post(i, norm_ffn, w_ff1, w_ff2, norm_ple, w_ple_gate, w_ple, norm_f)
        final = i == depth - 1
        if i % 2 == 0:
            w = _prep_layer_ab(j, norm_mix[i], w_in_ab, conv_w, q_norm, w_uq, kv_norm, w_uk, w_uv)
            w_out = w_out_ab[j].astype(BF16)
            w_outs = [w_out[:conv_width], w_out[conv_width:]]
            yconv, ckv, kpe, q, k, v, cstate = _ab_prompt(hp, tab_p, w, batch, seq, tm_p)
            yatt = _flash(q.reshape(batch, seq, -1), k.reshape(batch, seq, -1),
                          v.reshape(batch, seq, -1), scale, n_heads, tq)
            hp = _post(hp, [yconv, yatt.reshape(n_p, -1)], w_outs, pp[i], wpost, tm_p, final,
                       "post_prompt")
            conv_p.append(cstate[:, SUBLANES - 2:, :])
            ckv_p.append(ckv.reshape(batch, seq, -1))
            kpe_p.append(kpe.reshape(batch, seq, -1))
            state = state_conv[j].transpose(1, 0, 2)
            yconv, ckv, kpe, q, qlat, up = _ab_sample(hs, tab_s, w, state, n_t)
            kv_rank = ckv.shape[1]
            ql = qlat.reshape(n_heads, n_t, db, kv_rank).transpose(2, 1, 0, 3).reshape(
                db, n_t * n_heads, kv_rank)
            qp = q.reshape(n_t, db, n_heads, LANES)[..., nope:nope + rope].transpose(
                1, 0, 2, 3).reshape(db, n_t * n_heads, rope)
            ckv_b = unmajor(ckv)
            kpe_b = unmajor(kpe)
            padr = ((0, 0), (0, SUBLANES - n_t), (0, 0))
            yatt = _paged(page_table, ql, qp, jnp.pad(ckv_b, padr), jnp.pad(kpe_b, padr),
                          w['w_v'], cache_ckv[j], cache_kpe[j], scale, n_heads)
            yatt = yatt.transpose(1, 0, 2).reshape(n_s, -1).astype(BF16)
            hs = _post(hs, [yconv, yatt], w_outs, ps[i], wpost, n_s, final, "post_sample")
            conv_s.append(unmajor(up)[:, n_t - 2:, :])
            ckv_s.append(ckv_b)
            kpe_s.append(kpe_b)
        else:
            w = _prep_layer_c(j, norm_mix[i], w_in_c, v_norm, w_s, b_s, n_t)
            w_outs = [w_out_c[j].astype(BF16)]
            m = _gmlp_prompt(hp, w, tm_p, w['chunk'])
            hp = _post(hp, [m], w_outs, pp[i], wpost, tm_p, final, "post_prompt")
            m, v = _gmlp_sample(hs, w, n_t)
            hs = _post(hs, [m], w_outs, ps[i], wpost, n_s, final, "post_sample")
            v_s.append(unmajor(v))

    y_prompt = hp.reshape(batch, seq, d)
    y_sample = unmajor(hs)
    return (y_prompt, y_sample, jnp.stack(conv_p), jnp.stack(conv_s), jnp.stack(ckv_p),
            jnp.stack(kpe_p), jnp.stack(ckv_s), jnp.stack(kpe_s), jnp.stack(v_s))
```

```python
import functools
import math

import jax
import jax.numpy as jnp
from jax import lax
from jax.experimental import pallas as pl
from jax.experimental.pallas import tpu as pltpu

F32 = jnp.float32
BF16 = jnp.bfloat16

EPS = 1e-6
ROPE_THETA = 10000.0
LANES = 128
SUBLANES = 8
VMEM_LIMIT_BYTES = 56 * 1024 * 1024
PAGES_PER_CHUNK = 32
PAGES_PER_PIECE = 8
N_STREAMS = 2
SUM_ROWS = 16
MASK_VALUE = -1e30
LOG2_E = math.log2(math.e)


def _rms(x, g):
    ms = jnp.mean(x * x, axis=-1, keepdims=True)
    return x * lax.rsqrt(ms + EPS) * g


def _const_spec(shape):
    nd = len(shape)
    return pl.BlockSpec(shape, lambda *_: (0,) * nd, pipeline_mode=pl.Buffered(1))


def _params(semantics):
    return pltpu.CompilerParams(dimension_semantics=semantics,
                                vmem_limit_bytes=VMEM_LIMIT_BYTES)


def _ab_project(x_ref, tab_ref, gmix_ref, win_ref, gq_ref, gkv_ref, cw_ref, rope):
    cw = cw_ref.shape[1]
    q_rank = gq_ref.shape[1]
    kv_rank = gkv_ref.shape[1]
    a = _rms(x_ref[...], gmix_ref[...]).astype(BF16)
    z = jnp.dot(a, win_ref[...], preferred_element_type=F32)
    o_q = 3 * cw
    o_kv = o_q + q_rank
    o_pe = o_kv + kv_rank
    up = z[:, cw:2 * cw] * z[:, 0:cw]
    gb = z[:, 2 * cw:o_q]
    ckv_n = _rms(z[:, o_kv:o_pe], gkv_ref[...])
    prod = z[:, o_pe:o_pe + LANES] * tab_ref[:, 0:LANES]
    kr = prod + pltpu.roll(prod, LANES - rope, axis=1)
    qn = _rms(z[:, o_q:o_kv], gq_ref[...]).astype(BF16)
    return up, gb, ckv_n, kr, qn


NT_DIMS = (((1,), (1,)), ((), ()))


def _ab_prompt_kernel(x_ref, tab_ref, gmix_ref, win_ref, gq_ref, gkv_ref, cw_ref,
                      tabt_ref, wqt_ref, wk_ref, ek_ref, wvt_ref,
                      yconv_ref, ckv_ref, kpe_ref, qt_ref, k_ref, vt_ref, cstate_ref,
                      carry_ref, *, tiles_per_seq):
    i = pl.program_id(0)

    @pl.when(i % tiles_per_seq == 0)
    def _():
        carry_ref[...] = jnp.zeros_like(carry_ref)

    up, gb, ckv_n, kr, qn = _ab_project(
        x_ref, tab_ref, gmix_ref, win_ref, gq_ref, gkv_ref, cw_ref, kpe_ref.shape[1])
    tm = up.shape[0]
    c0 = carry_ref[SUBLANES - 2:SUBLANES - 1, :]
    c1 = carry_ref[SUBLANES - 1:SUBLANES, :]
    row = lax.broadcasted_iota(jnp.int32, (tm, 1), 0)
    um1 = jnp.where(row == 0, c1, pltpu.roll(up, 1, axis=0))
    um2 = jnp.where(row == 0, c0, jnp.where(row == 1, c1, pltpu.roll(up, 2, axis=0)))
    cw = cw_ref[...]
    conv = cw[0:1, :] * um2 + cw[1:2, :] * um1 + cw[2:3, :] * up
    yconv_ref[...] = (gb * conv).astype(yconv_ref.dtype)
    tail = up[tm - SUBLANES:, :]
    carry_ref[...] = tail
    cstate_ref[0] = tail

    ckv_ref[...] = ckv_n
    kpe_ref[...] = kr[:, 0:kpe_ref.shape[1]]
    q2t = lax.dot_general(wqt_ref[...], qn, NT_DIMS, preferred_element_type=F32)
    half = q2t.shape[0] // 2
    n_heads = half // LANES
    cpat = jnp.concatenate([tabt_ref[0:LANES, :]] * n_heads, axis=0)
    spat = jnp.concatenate([tabt_ref[LANES:2 * LANES, :]] * n_heads, axis=0)
    qt_ref[0] = (q2t[:half] * cpat + q2t[half:] * spat).astype(qt_ref.dtype)
    ckv_b = ckv_n.astype(BF16)
    k_cat = (jnp.dot(ckv_b, wk_ref[...], preferred_element_type=F32)
             + jnp.dot(kr.astype(BF16), ek_ref[...], preferred_element_type=F32))
    k_ref[...] = k_cat.astype(k_ref.dtype)
    vt_ref[0] = lax.dot_general(wvt_ref[...], ckv_b, NT_DIMS,
                                preferred_element_type=F32).astype(vt_ref.dtype)


def _ab_sample_kernel(x_ref, tab_ref, gmix_ref, win_ref, gq_ref, gkv_ref, cw_ref,
                      wq_ref, wukt_ref, state_ref,
                      yconv_ref, ckv_ref, kpe_ref, q_ref, qlat_ref, up_ref,
                      prev1_ref, prev2_ref):
    t = pl.program_id(0)

    @pl.when(t == 0)
    def _():
        prev2_ref[...] = state_ref[0]
        prev1_ref[...] = state_ref[1]

    up, gb, ckv_n, kr, qn = _ab_project(
        x_ref, tab_ref, gmix_ref, win_ref, gq_ref, gkv_ref, cw_ref, kpe_ref.shape[1])
    q2 = jnp.dot(qn, wq_ref[...], preferred_element_type=F32)
    half = q2.shape[1] // 2
    cpat = jnp.concatenate([tab_ref[:, LANES:2 * LANES]] * (half // LANES), axis=1)
    spat = jnp.concatenate([tab_ref[:, 2 * LANES:3 * LANES]] * (half // LANES), axis=1)
    q_r = q2[:, :half] * cpat + q2[:, half:] * spat
    cw = cw_ref[...]
    prev1 = prev1_ref[...]
    conv = cw[0:1, :] * prev2_ref[...] + cw[1:2, :] * prev1 + cw[2:3, :] * up
    yconv_ref[...] = (gb * conv).astype(yconv_ref.dtype)
    up_ref[...] = up
    prev2_ref[...] = prev1
    prev1_ref[...] = up

    ckv_ref[...] = ckv_n
    kpe_ref[...] = kr[:, 0:kpe_ref.shape[1]]
    q_b = q_r.astype(BF16)
    q_ref[...] = q_b
    for h in range(qlat_ref.shape[0]):
        qlat_ref[h] = jnp.dot(q_b[:, h * LANES:(h + 1) * LANES], wukt_ref[h],
                              preferred_element_type=F32).astype(qlat_ref.dtype)


def _flash_kernel(qi_ref, kj_ref, qt_ref, k_ref, vt_ref, o_ref, m_ref, acc_ref, *,
                  c_exp, n_heads, v_dim):
    t = pl.program_id(1)
    qi = qi_ref[t]
    kj = kj_ref[t]
    tk = k_ref.shape[1]
    tq = qt_ref.shape[2]

    @pl.when(kj == 0)
    def _():
        m_ref[...] = jnp.full_like(m_ref, MASK_VALUE)
        acc_ref[...] = jnp.zeros_like(acc_ref)

    def step(masked):
        ones = jnp.ones((SUM_ROWS, tk), BF16)
        if masked:
            k_id = lax.broadcasted_iota(jnp.int32, (tk, tq), 0)
            q_id = lax.broadcasted_iota(jnp.int32, (tk, tq), 1)
            keep = k_id <= q_id
        for h in range(n_heads):
            k = k_ref[0, :, h * LANES:(h + 1) * LANES]
            qt = qt_ref[0, h * LANES:(h + 1) * LANES, :]
            st = jnp.dot(k, qt, preferred_element_type=F32)
            if masked:
                st = jnp.where(keep, st, MASK_VALUE)
            m_old = m_ref[h]
            m_new = jnp.maximum(m_old, jnp.max(st, axis=0, keepdims=True))
            alpha = jnp.exp2((m_old - m_new) * c_exp)
            pt = jnp.exp2((st - m_new) * c_exp).astype(BF16)
            vx = jnp.concatenate([vt_ref[0, h * v_dim:(h + 1) * v_dim, :], ones], axis=0)
            acc_ref[h] = alpha * acc_ref[h] + jnp.dot(vx, pt, preferred_element_type=F32)
            m_ref[h] = m_new

    @pl.when(kj < qi)
    def _():
        step(False)

    @pl.when(kj == qi)
    def _():
        step(True)
        ot = jnp.concatenate(
            [acc_ref[h, 0:v_dim, :] / acc_ref[h, v_dim:v_dim + 1, :] for h in range(n_heads)],
            axis=0)
        o_ref[0] = ot.T.astype(o_ref.dtype)


def _paged_kernel(pt_ref, ql_ref, qp_ref, kn_ref, pn_ref, wuv_ref, pool_c, pool_pt,
                  o_ref, cbuf, pbuf, sem_c, sem_p, *, c_exp, n_pages, n_heads, v_dim, piece):
    b = pl.program_id(0)
    n_b = pl.num_programs(0)
    cp = cbuf.shape[1]
    page = cbuf.shape[2]
    n_chunks = n_pages // cp

    def page_copies(pg, k, slot):
        return (pltpu.make_async_copy(pool_c.at[pg], cbuf.at[slot, k], sem_c.at[slot]),
                pltpu.make_async_copy(pool_pt.at[pg], pbuf.at[slot, :, pl.ds(k * page, page)],
                                      sem_p.at[slot]))

    def start_chunk(g, slot):
        for k in range(cp):
            for cpy in page_copies(pt_ref[g * cp + k], k, slot):
                cpy.start()

    def wait_chunk(slot):
        for k in range(cp):
            for cpy in page_copies(0, k, slot):
                cpy.wait()

    last_g = n_b * n_chunks - 1

    @pl.when(b == 0)
    def _():
        start_chunk(0, 0)

    ql = ql_ref[0]
    qp = qp_ref[0]
    rows = ql.shape[0]

    kn = kn_ref[0].astype(BF16)
    pn = pn_ref[0].astype(BF16)
    s0 = (lax.dot_general(ql, kn, NT_DIMS, preferred_element_type=F32)
          + lax.dot_general(qp, pn, NT_DIMS, preferred_element_type=F32))
    r_id = lax.broadcasted_iota(jnp.int32, s0.shape, 0)
    c_id = lax.broadcasted_iota(jnp.int32, s0.shape, 1)
    s0 = jnp.where(c_id <= r_id // n_heads, s0, MASK_VALUE)
    m0 = jnp.max(s0, axis=1, keepdims=True)
    p0 = jnp.exp2((s0 - m0) * c_exp)
    l0 = jnp.sum(p0, axis=1, keepdims=True)
    acc0 = jnp.dot(p0.astype(BF16), kn, preferred_element_type=F32)

    def latent_piece(u, slot):
        kc = cbuf[slot, u * piece:(u + 1) * piece].reshape(piece * page, cbuf.shape[3])
        return kc.astype(BF16)

    def scores(u, slot):
        kpt = pbuf[slot, :, u * piece * page:(u + 1) * piece * page].astype(BF16)
        return (lax.dot_general(ql, latent_piece(u, slot), NT_DIMS, preferred_element_type=F32)
                + jnp.dot(qp, kpt, preferred_element_type=F32))

    def absorb(state, s, u, slot):
        m_run, l_run, acc = state
        m_new = jnp.maximum(m_run, jnp.max(s, axis=1, keepdims=True))
        alpha = jnp.exp2((m_run - m_new) * c_exp)
        p = jnp.exp2((s - m_new) * c_exp)
        l_new = alpha * l_run + jnp.sum(p, axis=1, keepdims=True)
        acc = alpha * acc + jnp.dot(p.astype(BF16), latent_piece(u, slot),
                                    preferred_element_type=F32)
        return m_new, l_new, acc

    def body(c, streams):
        g = b * n_chunks + c
        slot = g % 2
        wait_chunk(slot)
        start_chunk(jnp.minimum(g + 1, last_g), 1 - slot)
        streams = list(streams)
        s_all = [scores(u, slot) for u in range(cp // piece)]
        for u, s in enumerate(s_all):
            streams[u % N_STREAMS] = absorb(streams[u % N_STREAMS], s, u, slot)
        return tuple(streams)

    empty = (jnp.full_like(m0, MASK_VALUE), jnp.zeros_like(l0), jnp.zeros_like(acc0))
    streams = lax.fori_loop(0, n_chunks, body, ((m0, l0, acc0),) + (empty,) * (N_STREAMS - 1))

    @pl.when(b == n_b - 1)
    def _():
        wait_chunk(1 - last_g % 2)

    m_fin = functools.reduce(jnp.maximum, [st[0] for st in streams])
    weights = [jnp.exp2((st[0] - m_fin) * c_exp) for st in streams]
    l_fin = sum(wt * st[1] for wt, st in zip(weights, streams))
    acc = sum(wt * st[2] for wt, st in zip(weights, streams))
    o_lat = (acc / l_fin).astype(BF16)
    full = jnp.dot(o_lat, wuv_ref[...], preferred_element_type=F32)
    r_id = lax.broadcasted_iota(jnp.int32, full.shape, 0)
    c_id = lax.broadcasted_iota(jnp.int32, full.shape, 1)
    full = jnp.where(c_id // v_dim == r_id % n_heads, full, 0.0)
    o_ref[0] = jnp.sum(full.reshape(rows // n_heads, n_heads, full.shape[1]), axis=1)


def _gelu(x):
    return 0.5 * x * (1.0 + lax.erf(x * (1.0 / math.sqrt(2.0))))


def _gmlp_uv(x, g_mix, w_in, g_v):
    a = _rms(x, g_mix).astype(BF16)
    z = _gelu(jnp.dot(a, w_in, preferred_element_type=F32))
    width = z.shape[1] // 2
    return z[:, :width], _rms(z[:, width:], g_v)


def _gmlp_prompt_kernel(x_ref, gmix_ref, win_ref, gv_ref, ws_ref, bias_ref, m_ref, *, chunk):
    u, v = _gmlp_uv(x_ref[...], gmix_ref[...], win_ref[...], gv_ref[...])
    vb = v.astype(BF16)
    n_groups = ws_ref.shape[0]
    r_id = lax.broadcasted_iota(jnp.int32, (chunk, chunk), 0)
    c_id = lax.broadcasted_iota(jnp.int32, (chunk, chunk), 1)
    tril = c_id <= r_id
    bias = bias_ref[...]
    for g in range(n_groups):
        wg = jnp.where(tril, ws_ref[g], 0.0).astype(BF16)
        gs = slice(g * LANES, (g + 1) * LANES)
        for c in range(x_ref.shape[0] // chunk):
            rs = slice(c * chunk, (c + 1) * chunk)
            s = jnp.dot(wg, vb[rs, gs], preferred_element_type=F32) + bias[:, gs]
            m_ref[rs, gs] = (u[rs, gs] * s).astype(m_ref.dtype)


def _gmlp_sample_kernel(x_ref, gmix_ref, win_ref, gv_ref, ws_ref, bias_ref, m_ref, v_ref, *, n_t):
    u, v = _gmlp_uv(x_ref[...], gmix_ref[...], win_ref[...], gv_ref[...])
    v_ref[...] = v
    db = x_ref.shape[0] // n_t
    vb = v.astype(BF16).astype(F32)
    for t in range(n_t):
        s = bias_ref[t:t + 1, :]
        for k in range(t + 1):
            w = ws_ref[t, k:k + 1, :].astype(BF16).astype(F32)
            s = s + w * vb[k * db:(k + 1) * db, :]
        m_ref[t * db:(t + 1) * db, :] = (u[t * db:(t + 1) * db, :] * s).astype(m_ref.dtype)


def _post_kernel(*refs, n_mix, ff_chunk, final):
    h_ref = refs[0]
    mix_refs = refs[1:1 + n_mix]
    wout_refs = refs[1 + n_mix:1 + 2 * n_mix]
    (p_ref, gffn_ref, w1_ref, w2_ref, gple_ref, wg_ref, wple_ref, gf_ref, out_ref) = refs[1 + 2 * n_mix:]
    h = h_ref[...]
    for m_ref, w_ref in zip(mix_refs, wout_refs):
        h = h + jnp.dot(m_ref[...], w_ref[...], preferred_element_type=F32)
    a = _rms(h, gffn_ref[...]).astype(BF16)
    d_ff = w1_ref.shape[1]
    y = None
    for c in range(d_ff // ff_chunk):
        cs = slice(c * ff_chunk, (c + 1) * ff_chunk)
        hid = jnp.maximum(jnp.dot(a, w1_ref[:, cs], preferred_element_type=F32), 0.0)
        part = jnp.dot((hid * hid).astype(BF16), w2_ref[cs, :], preferred_element_type=F32)
        y = part if y is None else y + part
    h = h + y
    a = _rms(h, gple_ref[...]).astype(BF16)
    gate = jax.nn.sigmoid(jnp.dot(a, wg_ref[...], preferred_element_type=F32))
    pe = jnp.dot(p_ref[...].astype(BF16), wple_ref[...], preferred_element_type=F32)
    h = h + gate * pe
    if final:
        h = _rms(h, gf_ref[...])
    out_ref[...] = h


def _row_spec(tm, width):
    return pl.BlockSpec((tm, width), lambda i: (i, 0))


def _ab_common_specs(tm, d, tab_width, tab_map, w):
    return [
        _row_spec(tm, d),
        pl.BlockSpec((tm, tab_width), tab_map),
        _const_spec((1, d)),
        _const_spec(w['w_in'].shape),
        _const_spec(w['g_q'].shape),
        _const_spec(w['g_kv'].shape),
        _const_spec(w['conv_w'].shape),
    ]


def _ab_prompt(h, tab, tab_t, w, batch, seq, tm):
    n, d = h.shape
    tiles_per_seq = seq // tm
    conv_w = w['conv_w'].shape[1]
    kv_rank = w['g_kv'].shape[1]
    rope = w['rope_dim']
    qw = w['w_qt'].shape[0] // 2
    vw = w['w_vt'].shape[0]
    in_specs = _ab_common_specs(tm, d, LANES, lambda i: (i % tiles_per_seq, 0), w) + [
        pl.BlockSpec((2 * LANES, tm), lambda i: (0, i % tiles_per_seq)),
        _const_spec(w['w_qt'].shape), _const_spec(w['w_k'].shape), _const_spec(w['e_k'].shape),
        _const_spec(w['w_vt'].shape)]
    out_shape = (
        jax.ShapeDtypeStruct((n, conv_w), BF16),
        jax.ShapeDtypeStruct((n, kv_rank), F32),
        jax.ShapeDtypeStruct((n, rope), F32),
        jax.ShapeDtypeStruct((batch, qw, seq), BF16),
        jax.ShapeDtypeStruct((n, qw), BF16),
        jax.ShapeDtypeStruct((batch, vw, seq), BF16),
        jax.ShapeDtypeStruct((batch, SUBLANES, conv_w), F32),
    )

    def feature_major(width):
        return pl.BlockSpec((1, width, tm), lambda i: (i // tiles_per_seq, 0, i % tiles_per_seq))

    out_specs = (
        _row_spec(tm, conv_w), _row_spec(tm, kv_rank), _row_spec(tm, rope),
        feature_major(qw), _row_spec(tm, qw), feature_major(vw),
        pl.BlockSpec((1, SUBLANES, conv_w), lambda i: (i // tiles_per_seq, 0, 0)),
    )
    return pl.pallas_call(
        functools.partial(_ab_prompt_kernel, tiles_per_seq=tiles_per_seq),
        grid=(n // tm,), in_specs=in_specs, out_specs=out_specs, out_shape=out_shape,
        scratch_shapes=[pltpu.VMEM((SUBLANES, conv_w), F32)],
        compiler_params=_params(("arbitrary",)), name="ab_prompt",
    )(h, tab, w['g_mix'], w['w_in'], w['g_q'], w['g_kv'], w['conv_w'],
      tab_t, w['w_qt'], w['w_k'], w['e_k'], w['w_vt'])


def _ab_sample(h, tab, w, state, n_t):
    n, d = h.shape
    db = n // n_t
    conv_w = w['conv_w'].shape[1]
    kv_rank = w['g_kv'].shape[1]
    rope = w['rope_dim']
    qw = w['w_q'].shape[1] // 2
    n_heads = w['w_ukt'].shape[0]
    in_specs = _ab_common_specs(db, d, 3 * LANES, lambda i: (i, 0), w) + [
        _const_spec(w['w_q'].shape), _const_spec(w['w_ukt'].shape), _const_spec(state.shape)]
    out_shape = (
        jax.ShapeDtypeStruct((n, conv_w), BF16),
        jax.ShapeDtypeStruct((n, kv_rank), F32),
        jax.ShapeDtypeStruct((n, rope), F32),
        jax.ShapeDtypeStruct((n, qw), BF16),
        jax.ShapeDtypeStruct((n_heads, n, kv_rank), BF16),
        jax.ShapeDtypeStruct((n, conv_w), F32),
    )
    out_specs = (
        _row_spec(db, conv_w), _row_spec(db, kv_rank), _row_spec(db, rope), _row_spec(db, qw),
        pl.BlockSpec((n_heads, db, kv_rank), lambda i: (0, i, 0)),
        _row_spec(db, conv_w),
    )
    return pl.pallas_call(
        _ab_sample_kernel,
        grid=(n_t,), in_specs=in_specs, out_specs=out_specs, out_shape=out_shape,
        scratch_shapes=[pltpu.VMEM((db, conv_w), F32), pltpu.VMEM((db, conv_w), F32)],
        compiler_params=_params(("arbitrary",)), name="ab_sample",
    )(h, tab, w['g_mix'], w['w_in'], w['g_q'], w['g_kv'], w['conv_w'],
      w['w_q'], w['w_ukt'], state)


def _flash(qt, k, vt, scale, n_heads, tq):
    batch, seq, qw = k.shape
    vw = vt.shape[1]
    v_dim = vw // n_heads
    nq = seq // tq
    pairs = [(i, j) for i in range(nq) for j in range(i + 1)]
    qi = jnp.asarray([p[0] for p in pairs], jnp.int32)
    kj = jnp.asarray([p[1] for p in pairs], jnp.int32)
    grid_spec = pltpu.PrefetchScalarGridSpec(
        num_scalar_prefetch=2,
        grid=(batch, len(pairs)),
        in_specs=[
            pl.BlockSpec((1, qw, tq), lambda b, t, qi, kj: (b, 0, qi[t])),
            pl.BlockSpec((1, tq, qw), lambda b, t, qi, kj: (b, kj[t], 0)),
            pl.BlockSpec((1, vw, tq), lambda b, t, qi, kj: (b, 0, kj[t])),
        ],
        out_specs=pl.BlockSpec((1, tq, vw), lambda b, t, qi, kj: (b, qi[t], 0)),
        scratch_shapes=[pltpu.VMEM((n_heads, 1, tq), F32),
                        pltpu.VMEM((n_heads, v_dim + SUM_ROWS, tq), F32)],
    )
    return pl.pallas_call(
        functools.partial(_flash_kernel, c_exp=scale * LOG2_E, n_heads=n_heads, v_dim=v_dim),
        grid_spec=grid_spec,
        out_shape=jax.ShapeDtypeStruct((batch, seq, vw), BF16),
        compiler_params=_params(("arbitrary", "arbitrary")), name="flash_prompt",
    )(qi, kj, qt, k, vt)


def _paged(page_table, ql, qp, kn, pn, w_uv, pool_c, pool_pt, scale, n_heads):
    db, rows, kv_rank = ql.shape
    rope = qp.shape[2]
    n_pages = page_table.shape[1]
    page = pool_c.shape[1]
    vw = w_uv.shape[1]
    n_t = rows // n_heads
    cp = min(PAGES_PER_CHUNK, n_pages)
    piece = min(PAGES_PER_PIECE, cp)
    assert n_pages % cp == 0 and cp % piece == 0
    grid_spec = pltpu.PrefetchScalarGridSpec(
        num_scalar_prefetch=1,
        grid=(db,),
        in_specs=[
            pl.BlockSpec((1, rows, kv_rank), lambda b, pt: (b, 0, 0)),
            pl.BlockSpec((1, rows, rope), lambda b, pt: (b, 0, 0)),
            pl.BlockSpec((1, SUBLANES, kv_rank), lambda b, pt: (b, 0, 0)),
            pl.BlockSpec((1, SUBLANES, rope), lambda b, pt: (b, 0, 0)),
            pl.BlockSpec(w_uv.shape, lambda b, pt: (0, 0)),
            pl.BlockSpec(memory_space=pl.ANY),
            pl.BlockSpec(memory_space=pl.ANY),
        ],
        out_specs=pl.BlockSpec((1, n_t, vw), lambda b, pt: (b, 0, 0)),
        scratch_shapes=[
            pltpu.VMEM((2, cp, page, kv_rank), F32),
            pltpu.VMEM((2, rope, cp * page), F32),
            pltpu.SemaphoreType.DMA((2,)),
            pltpu.SemaphoreType.DMA((2,)),
        ],
    )
    return pl.pallas_call(
        functools.partial(_paged_kernel, c_exp=scale * LOG2_E, n_pages=n_pages, n_heads=n_heads,
                          v_dim=vw // n_heads, piece=piece),
        grid_spec=grid_spec,
        out_shape=jax.ShapeDtypeStruct((db, n_t, vw), F32),
        compiler_params=_params(("arbitrary",)), name="paged_attn",
    )(page_table.reshape(-1), ql, qp, kn, pn, w_uv, pool_c, pool_pt)


def _gmlp_prompt(h, w, tm, chunk):
    n, d = h.shape
    width = w['w_in_c'].shape[1] // 2
    return pl.pallas_call(
        functools.partial(_gmlp_prompt_kernel, chunk=chunk),
        grid=(n // tm,),
        in_specs=[_row_spec(tm, d), _const_spec((1, d)), _const_spec(w['w_in_c'].shape),
                  _const_spec((1, width)), _const_spec(w['w_s'].shape),
                  _const_spec(w['bias_full'].shape)],
        out_specs=_row_spec(tm, width),
        out_shape=jax.ShapeDtypeStruct((n, width), BF16),
        compiler_params=_params(("arbitrary",)), name="gmlp_prompt",
    )(h, w['g_mix'], w['w_in_c'], w['g_v'], w['w_s'], w['bias_full'])


def _gmlp_sample(h, w, n_t):
    n, d = h.shape
    width = w['w_in_c'].shape[1] // 2
    return pl.pallas_call(
        functools.partial(_gmlp_sample_kernel, n_t=n_t),
        grid=(1,),
        in_specs=[_const_spec((n, d)), _const_spec((1, d)), _const_spec(w['w_in_c'].shape),
                  _const_spec((1, width)), _const_spec(w['ws_small'].shape),
                  _const_spec(w['bias_small'].shape)],
        out_specs=(_const_spec((n, width)), _const_spec((n, width))),
        out_shape=(jax.ShapeDtypeStruct((n, width), BF16), jax.ShapeDtypeStruct((n, width), F32)),
        compiler_params=_params(("arbitrary",)), name="gmlp_sample",
    )(h, w['g_mix'], w['w_in_c'], w['g_v'], w['ws_small'], w['bias_small'])


def _post(h, mixes, w_outs, p, w, tm, final, name):
    n, d = h.shape
    n_mix = len(mixes)
    ff_chunk = min(1024, w['w1'].shape[1])
    in_specs = ([_row_spec(tm, d)] + [_row_spec(tm, m.shape[1]) for m in mixes]
                + [_const_spec(wo.shape) for wo in w_outs]
                + [_row_spec(tm, p.shape[1]), _const_spec((1, d)), _const_spec(w['w1'].shape),
                   _const_spec(w['w2'].shape), _const_spec((1, d)), _const_spec(w['w_gate'].shape),
                   _const_spec(w['w_ple'].shape), _const_spec((1, d))])
    return pl.pallas_call(
        functools.partial(_post_kernel, n_mix=n_mix, ff_chunk=ff_chunk, final=final),
        grid=(n // tm,), in_specs=in_specs, out_specs=_row_spec(tm, d),
        out_shape=jax.ShapeDtypeStruct((n, d), F32),
        compiler_params=_params(("arbitrary",)), name=name,
    )(h, *mixes, *w_outs, p, w['g_ffn'], w['w1'], w['w2'], w['g_ple'], w['w_gate'], w['w_ple'],
      w['g_f'])


def _rope_table(pos, rope_dim):
    inv = ROPE_THETA ** (-jnp.arange(0, rope_dim, 2, dtype=F32) / rope_dim)
    ang = pos.astype(F32)[:, None] * inv[None, :]
    c, s = jnp.cos(ang), jnp.sin(ang)
    t = pos.shape[0]
    cc = jnp.concatenate([c, c], axis=1)
    ss = jnp.concatenate([-s, s], axis=1)
    zeros = lambda w_: jnp.zeros((t, w_), F32)
    nope = LANES - 2 * rope_dim
    key_tab = jnp.concatenate([cc, ss, zeros(LANES - 2 * rope_dim)], axis=1)
    q_cos = jnp.concatenate([jnp.ones((t, nope), F32), cc, zeros(LANES - nope - rope_dim)], axis=1)
    q_sin = jnp.concatenate([zeros(nope), ss, zeros(LANES - nope - rope_dim)], axis=1)
    return jnp.concatenate([key_tab, q_cos, q_sin], axis=1)


def _prep_layer_ab(j, norm_mix_i, w_in_ab, conv_w, q_norm, w_uq, kv_norm, w_uk, w_uv):
    d = w_in_ab.shape[1]
    kv_rank, n_heads, nope = w_uk.shape[1:]
    rope = w_uq.shape[2] // n_heads - nope
    v_dim = w_uv.shape[3]
    half = rope // 2
    in_ab = w_in_ab.shape[2]
    w_in = w_in_ab[j]
    kpe0 = in_ab - rope
    pad = (-(in_ab + rope)) % (2 * LANES)
    w_in_p = jnp.concatenate(
        [w_in, w_in[:, kpe0 + half:], w_in[:, kpe0:kpe0 + half], jnp.zeros((d, pad), F32)], axis=1)
    q_rank = w_uq.shape[1]
    uq = w_uq[j].reshape(q_rank, n_heads, nope + rope)
    zpad = jnp.zeros((q_rank, n_heads, LANES - nope - rope), F32)
    wq_a = jnp.concatenate([uq, zpad], axis=2)
    wq_b = jnp.concatenate([jnp.zeros((q_rank, n_heads, nope), F32), uq[:, :, nope + half:],
                            uq[:, :, nope:nope + half], zpad], axis=2)
    w_q = jnp.concatenate([wq_a.reshape(q_rank, -1), wq_b.reshape(q_rank, -1)], axis=1)
    uk = w_uk[j]
    w_k = jnp.concatenate([uk, jnp.zeros((kv_rank, n_heads, LANES - nope), F32)], axis=2)
    e_k = jnp.zeros((LANES, n_heads, LANES), F32)
    idx = jnp.arange(rope)
    e_k = e_k.at[idx, :, nope + idx].set(1.0)
    w_ukt = jnp.concatenate([uk.transpose(1, 2, 0),
                             jnp.zeros((n_heads, LANES - nope, kv_rank), F32)], axis=1)
    return dict(
        g_mix=norm_mix_i.reshape(1, -1), w_in=w_in_p.astype(BF16), g_q=q_norm[j].reshape(1, -1),
        w_q=w_q.astype(BF16), w_qt=w_q.T.astype(BF16), g_kv=kv_norm[j].reshape(1, -1),
        conv_w=conv_w[j],
        w_k=w_k.reshape(kv_rank, -1).astype(BF16), e_k=e_k.reshape(LANES, -1).astype(BF16),
        w_v=w_uv[j].reshape(kv_rank, n_heads * v_dim).astype(BF16),
        w_vt=w_uv[j].reshape(kv_rank, n_heads * v_dim).T.astype(BF16), w_ukt=w_ukt.astype(BF16),
        rope_dim=rope, n_heads=n_heads, nope=nope)


def _prep_layer_c(j, norm_mix_i, w_in_c, v_norm, w_s, b_s, n_t):
    n_groups, chunk = w_s.shape[1], w_s.shape[2]
    width = w_in_c.shape[2] // 2
    gw = width // n_groups
    bias_full = jnp.repeat(b_s[j].T, gw, axis=1)
    ws_small = jnp.repeat(w_s[j][:, :n_t, :n_t].transpose(1, 2, 0), gw, axis=2)
    return dict(g_mix=norm_mix_i.reshape(1, -1), w_in_c=w_in_c[j].astype(BF16),
                g_v=v_norm[j].reshape(1, -1), w_s=w_s[j], bias_full=bias_full,
                ws_small=ws_small, bias_small=bias_full[:n_t], chunk=chunk)


def _prep_post(i, norm_ffn, w_ff1, w_ff2, norm_ple, w_ple_gate, w_ple, norm_f):
    return dict(g_ffn=norm_ffn[i].reshape(1, -1), w1=w_ff1[i].astype(BF16), w2=w_ff2[i].astype(BF16),
                g_ple=norm_ple[i].reshape(1, -1), w_gate=w_ple_gate[i].astype(BF16),
                w_ple=w_ple[i].astype(BF16), g_f=norm_f.reshape(1, -1))


def _token_tile(n, cap):
    tm = min(cap, n)
    while n % tm:
        tm //= 2
    return tm


def kernel(x_prompt, x_sample, cache_ckv, cache_kpe, state_conv, page_table, p_prompt, p_sample, norm_mix, w_in_ab, conv_w, q_norm, w_uq, kv_norm, w_uk, w_uv, w_out_ab, w_in_c, v_norm, w_s, b_s, w_out_c, norm_ffn, w_ff1, w_ff2, norm_ple, w_ple_gate, w_ple, norm_f):
    depth = norm_mix.shape[0]
    batch, seq, d = x_prompt.shape
    db, n_t, _ = x_sample.shape
    n_pages = page_table.shape[1]
    page = cache_ckv.shape[2]
    past_len = n_pages * page
    n_heads, nope = w_uk.shape[2], w_uk.shape[3]
    rope = cache_kpe.shape[3]
    conv_width = conv_w.shape[2]
    scale = float(nope + rope) ** -0.5

    tab_p = _rope_table(jnp.arange(seq, dtype=jnp.int32), rope)
    tab_s = jnp.repeat(_rope_table(past_len + jnp.arange(n_t, dtype=jnp.int32), rope), db, axis=0)

    n_p = batch * seq
    n_s = db * n_t
    tm_p = _token_tile(seq, 512)
    tq = _token_tile(seq, 512)
    hp = x_prompt.reshape(n_p, d)
    hs = x_sample.transpose(1, 0, 2).reshape(n_s, d)
    pp = p_prompt.reshape(depth, n_p, -1)
    ps = p_sample.transpose(0, 2, 1, 3).reshape(depth, n_s, -1)

    def unmajor(a):
        return a.reshape(n_t, db, a.shape[-1]).transpose(1, 0, 2)

    conv_p, conv_s, ckv_p, kpe_p, ckv_s, kpe_s, v_s = [], [], [], [], [], [], []
    for i in range(depth):
        j = i // 2
        wpost = _prep_post(i, norm_ffn, w_ff1, w_ff2, norm_ple, w_ple_gate, w_ple, norm_f)
        final = i == depth - 1
        if i % 2 == 0:
            w = _prep_layer_ab(j, norm_mix[i], w_in_ab, conv_w, q_norm, w_uq, kv_norm, w_uk, w_uv)
            w_out = w_out_ab[j].astype(BF16)
            w_outs = [w_out[:conv_width], w_out[conv_width:]]
            yconv, ckv, kpe, qt, k, vt, cstate = _ab_prompt(
                hp, tab_p[:, :LANES], tab_p[:, LANES:].T, w, batch, seq, tm_p)
            yatt = _flash(qt, k.reshape(batch, seq, -1), vt, scale, n_heads, tq)
            hp = _post(hp, [yconv, yatt.reshape(n_p, -1)], w_outs, pp[i], wpost, tm_p, final,
                       "post_prompt")
            conv_p.append(cstate[:, SUBLANES - 2:, :])
            ckv_p.append(ckv.reshape(batch, seq, -1))
            kpe_p.append(kpe.reshape(batch, seq, -1))
            state = state_conv[j].transpose(1, 0, 2)
            yconv, ckv, kpe, q, qlat, up = _ab_sample(hs, tab_s, w, state, n_t)
            kv_rank = ckv.shape[1]
            ql = qlat.reshape(n_heads, n_t, db, kv_rank).transpose(2, 1, 0, 3).reshape(
                db, n_t * n_heads, kv_rank)
            qp = q.reshape(n_t, db, n_heads, LANES)[..., nope:nope + rope].transpose(
                1, 0, 2, 3).reshape(db, n_t * n_heads, rope)
            ckv_b = unmajor(ckv)
            kpe_b = unmajor(kpe)
            padr = ((0, 0), (0, SUBLANES - n_t), (0, 0))
            yatt = _paged(page_table, ql, qp, jnp.pad(ckv_b, padr), jnp.pad(kpe_b, padr),
                          w['w_v'], cache_ckv[j], jnp.swapaxes(cache_kpe[j], 1, 2), scale, n_heads)
            yatt = yatt.transpose(1, 0, 2).reshape(n_s, -1).astype(BF16)
            hs = _post(hs, [yconv, yatt], w_outs, ps[i], wpost, n_s, final, "post_sample")
            conv_s.append(unmajor(up)[:, n_t - 2:, :])
            ckv_s.append(ckv_b)
            kpe_s.append(kpe_b)
        else:
            w = _prep_layer_c(j, norm_mix[i], w_in_c, v_norm, w_s, b_s, n_t)
            w_outs = [w_out_c[j].astype(BF16)]
            m = _gmlp_prompt(hp, w, tm_p, w['chunk'])
            hp = _post(hp, [m], w_outs, pp[i], wpost, tm_p, final, "post_prompt")
            m, v = _gmlp_sample(hs, w, n_t)
            hs = _post(hs, [m], w_outs, ps[i], wpost, n_s, final, "post_sample")
            v_s.append(unmajor(v))

    y_prompt = hp.reshape(batch, seq, d)
    y_sample = unmajor(hs)
    return (y_prompt, y_sample, jnp.stack(conv_p), jnp.stack(conv_s), jnp.stack(ckv_p),
            jnp.stack(kpe_p), jnp.stack(ckv_s), jnp.stack(kpe_s), jnp.stack(v_s))
```

```python
import functools
import math

import jax
import jax.numpy as jnp
from jax import lax
from jax.experimental import pallas as pl
from jax.experimental.pallas import tpu as pltpu

F32 = jnp.float32
BF16 = jnp.bfloat16

EPS = 1e-6
ROPE_THETA = 10000.0
LANES = 128
SUBLANES = 8
VMEM_LIMIT_BYTES = 56 * 1024 * 1024
PAGES_PER_CHUNK = 32
PAGES_PER_PIECE = 8
N_STREAMS = 2
N_SLOTS = 3
QK_LOOKAHEAD = 3
SUM_ROWS = 16
MASK_VALUE = -1e30
LOG2_E = math.log2(math.e)


def _rms(x, g):
    ms = jnp.mean(x * x, axis=-1, keepdims=True)
    return x * lax.rsqrt(ms + EPS) * g


def _const_spec(shape):
    nd = len(shape)
    return pl.BlockSpec(shape, lambda *_: (0,) * nd, pipeline_mode=pl.Buffered(1))


def _params(semantics):
    return pltpu.CompilerParams(dimension_semantics=semantics,
                                vmem_limit_bytes=VMEM_LIMIT_BYTES)


def _ab_project(x_ref, tab_ref, gmix_ref, win_ref, gq_ref, gkv_ref, cw_ref, rope):
    cw = cw_ref.shape[1]
    q_rank = gq_ref.shape[1]
    kv_rank = gkv_ref.shape[1]
    a = _rms(x_ref[...], gmix_ref[...]).astype(BF16)
    z = jnp.dot(a, win_ref[...], preferred_element_type=F32)
    o_q = 3 * cw
    o_kv = o_q + q_rank
    o_pe = o_kv + kv_rank
    up = z[:, cw:2 * cw] * z[:, 0:cw]
    gb = z[:, 2 * cw:o_q]
    ckv_n = _rms(z[:, o_kv:o_pe], gkv_ref[...])
    prod = z[:, o_pe:o_pe + LANES] * tab_ref[:, 0:LANES]
    kr = prod + pltpu.roll(prod, LANES - rope, axis=1)
    qn = _rms(z[:, o_q:o_kv], gq_ref[...]).astype(BF16)
    return up, gb, ckv_n, kr, qn


NT_DIMS = (((1,), (1,)), ((), ()))


def _ab_prompt_kernel(x_ref, tab_ref, gmix_ref, win_ref, gq_ref, gkv_ref, cw_ref,
                      tabt_ref, wqt_ref, wk_ref, ek_ref, wvt_ref,
                      yconv_ref, ckv_ref, kpe_ref, qt_ref, k_ref, vt_ref, cstate_ref,
                      carry_ref, *, tiles_per_seq):
    i = pl.program_id(0)

    @pl.when(i % tiles_per_seq == 0)
    def _():
        carry_ref[...] = jnp.zeros_like(carry_ref)

    up, gb, ckv_n, kr, qn = _ab_project(
        x_ref, tab_ref, gmix_ref, win_ref, gq_ref, gkv_ref, cw_ref, kpe_ref.shape[1])
    tm = up.shape[0]
    c0 = carry_ref[SUBLANES - 2:SUBLANES - 1, :]
    c1 = carry_ref[SUBLANES - 1:SUBLANES, :]
    row = lax.broadcasted_iota(jnp.int32, (tm, 1), 0)
    um1 = jnp.where(row == 0, c1, pltpu.roll(up, 1, axis=0))
    um2 = jnp.where(row == 0, c0, jnp.where(row == 1, c1, pltpu.roll(up, 2, axis=0)))
    cw = cw_ref[...]
    conv = cw[0:1, :] * um2 + cw[1:2, :] * um1 + cw[2:3, :] * up
    yconv_ref[...] = (gb * conv).astype(yconv_ref.dtype)
    tail = up[tm - SUBLANES:, :]
    carry_ref[...] = tail
    cstate_ref[0] = tail

    ckv_ref[...] = ckv_n
    kpe_ref[...] = kr[:, 0:kpe_ref.shape[1]]
    q2t = lax.dot_general(wqt_ref[...], qn, NT_DIMS, preferred_element_type=F32)
    half = q2t.shape[0] // 2
    n_heads = half // LANES
    cpat = jnp.concatenate([tabt_ref[0:LANES, :]] * n_heads, axis=0)
    spat = jnp.concatenate([tabt_ref[LANES:2 * LANES, :]] * n_heads, axis=0)
    qt_ref[0] = (q2t[:half] * cpat + q2t[half:] * spat).astype(qt_ref.dtype)
    ckv_b = ckv_n.astype(BF16)
    k_cat = (jnp.dot(ckv_b, wk_ref[...], preferred_element_type=F32)
             + jnp.dot(kr.astype(BF16), ek_ref[...], preferred_element_type=F32))
    k_ref[...] = k_cat.astype(k_ref.dtype)
    vt_ref[0] = lax.dot_general(wvt_ref[...], ckv_b, NT_DIMS,
                                preferred_element_type=F32).astype(vt_ref.dtype)


def _ab_sample_kernel(x_ref, tab_ref, gmix_ref, win_ref, gq_ref, gkv_ref, cw_ref,
                      wq_ref, wukt_ref, state_ref,
                      yconv_ref, ckv_ref, kpe_ref, q_ref, qlat_ref, up_ref,
                      prev1_ref, prev2_ref):
    t = pl.program_id(0)

    @pl.when(t == 0)
    def _():
        prev2_ref[...] = state_ref[0]
        prev1_ref[...] = state_ref[1]

    up, gb, ckv_n, kr, qn = _ab_project(
        x_ref, tab_ref, gmix_ref, win_ref, gq_ref, gkv_ref, cw_ref, kpe_ref.shape[1])
    q2 = jnp.dot(qn, wq_ref[...], preferred_element_type=F32)
    half = q2.shape[1] // 2
    cpat = jnp.concatenate([tab_ref[:, LANES:2 * LANES]] * (half // LANES), axis=1)
    spat = jnp.concatenate([tab_ref[:, 2 * LANES:3 * LANES]] * (half // LANES), axis=1)
    q_r = q2[:, :half] * cpat + q2[:, half:] * spat
    cw = cw_ref[...]
    prev1 = prev1_ref[...]
    conv = cw[0:1, :] * prev2_ref[...] + cw[1:2, :] * prev1 + cw[2:3, :] * up
    yconv_ref[...] = (gb * conv).astype(yconv_ref.dtype)
    up_ref[...] = up
    prev2_ref[...] = prev1
    prev1_ref[...] = up

    ckv_ref[...] = ckv_n
    kpe_ref[...] = kr[:, 0:kpe_ref.shape[1]]
    q_b = q_r.astype(BF16)
    q_ref[...] = q_b
    for h in range(qlat_ref.shape[0]):
        qlat_ref[h] = jnp.dot(q_b[:, h * LANES:(h + 1) * LANES], wukt_ref[h],
                              preferred_element_type=F32).astype(qlat_ref.dtype)


def _flash_kernel(qi_ref, kj_ref, qt_ref, k_ref, vt_ref, o_ref, m_ref, acc_ref, *,
                  c_exp, n_heads, v_dim):
    t = pl.program_id(1)
    qi = qi_ref[t]
    kj = kj_ref[t]
    tk = k_ref.shape[1]
    tq = qt_ref.shape[2]

    @pl.when(kj == 0)
    def _():
        m_ref[...] = jnp.full_like(m_ref, MASK_VALUE)
        acc_ref[...] = jnp.zeros_like(acc_ref)

    def step(masked):
        ones = jnp.ones((SUM_ROWS, tk), BF16)
        if masked:
            k_id = lax.broadcasted_iota(jnp.int32, (tk, tq), 0)
            q_id = lax.broadcasted_iota(jnp.int32, (tk, tq), 1)
            keep = k_id <= q_id
        def scores_t(h):
            k = k_ref[0, :, h * LANES:(h + 1) * LANES]
            qt = qt_ref[0, h * LANES:(h + 1) * LANES, :]
            return jnp.dot(k, qt, preferred_element_type=F32)

        pending = [scores_t(h) for h in range(min(QK_LOOKAHEAD, n_heads))]
        for h in range(n_heads):
            st = pending.pop(0)
            if h + QK_LOOKAHEAD < n_heads:
                pending.append(scores_t(h + QK_LOOKAHEAD))
            if masked:
                st = jnp.where(keep, st, MASK_VALUE)
            m_old = m_ref[h]
            m_new = jnp.maximum(m_old, jnp.max(st, axis=0, keepdims=True))
            alpha = jnp.exp2((m_old - m_new) * c_exp)
            pt = jnp.exp2((st - m_new) * c_exp).astype(BF16)
            vx = jnp.concatenate([vt_ref[0, h * v_dim:(h + 1) * v_dim, :], ones], axis=0)
            acc_ref[h] = alpha * acc_ref[h] + jnp.dot(vx, pt, preferred_element_type=F32)
            m_ref[h] = m_new

    @pl.when(kj < qi)
    def _():
        step(False)

    @pl.when(kj == qi)
    def _():
        step(True)
        ot = jnp.concatenate(
            [acc_ref[h, 0:v_dim, :] / acc_ref[h, v_dim:v_dim + 1, :] for h in range(n_heads)],
            axis=0)
        o_ref[0] = ot.T.astype(o_ref.dtype)


def _paged_kernel(pt_ref, ql_ref, qp_ref, kn_ref, pn_ref, wuv_ref, pool_c, pool_pt,
                  o_ref, cbuf, pbuf, sem_c, sem_p, *, c_exp, n_pages, n_heads, v_dim, piece):
    b = pl.program_id(0)
    n_b = pl.num_programs(0)
    cp = cbuf.shape[1]
    page = cbuf.shape[2]
    n_chunks = n_pages // cp

    def page_copies(pg, k, slot):
        return (pltpu.make_async_copy(pool_c.at[pg], cbuf.at[slot, k], sem_c.at[slot]),
                pltpu.make_async_copy(pool_pt.at[pg], pbuf.at[slot, :, pl.ds(k * page, page)],
                                      sem_p.at[slot]))

    def start_chunk(g, slot):
        for k in range(cp):
            for cpy in page_copies(pt_ref[g * cp + k], k, slot):
                cpy.start()

    def wait_chunk(slot):
        for k in range(cp):
            for cpy in page_copies(0, k, slot):
                cpy.wait()

    last_g = n_b * n_chunks - 1
    n_slots = cbuf.shape[0]
    ahead = n_slots - 1

    @pl.when(b == 0)
    def _():
        for a in range(ahead):
            start_chunk(jnp.minimum(a, last_g), a)

    ql = ql_ref[0]
    qp = qp_ref[0]
    rows = ql.shape[0]

    kn = kn_ref[0].astype(BF16)
    pn = pn_ref[0].astype(BF16)
    s0 = (lax.dot_general(ql, kn, NT_DIMS, preferred_element_type=F32)
          + lax.dot_general(qp, pn, NT_DIMS, preferred_element_type=F32))
    r_id = lax.broadcasted_iota(jnp.int32, s0.shape, 0)
    c_id = lax.broadcasted_iota(jnp.int32, s0.shape, 1)
    s0 = jnp.where(c_id <= r_id // n_heads, s0, MASK_VALUE)
    m0 = jnp.max(s0, axis=1, keepdims=True)
    p0 = jnp.exp2((s0 - m0) * c_exp)
    l0 = jnp.sum(p0, axis=1, keepdims=True)
    acc0 = jnp.dot(p0.astype(BF16), kn, preferred_element_type=F32)

    def latent_piece(u, slot):
        kc = cbuf[slot, u * piece:(u + 1) * piece].reshape(piece * page, cbuf.shape[3])
        return kc.astype(BF16)

    def scores(u, slot):
        kpt = pbuf[slot, :, u * piece * page:(u + 1) * piece * page].astype(BF16)
        return (lax.dot_general(ql, latent_piece(u, slot), NT_DIMS, preferred_element_type=F32)
                + jnp.dot(qp, kpt, preferred_element_type=F32))

    def absorb(state, s, u, slot):
        m_run, l_run, acc = state
        m_new = jnp.maximum(m_run, jnp.max(s, axis=1, keepdims=True))
        alpha = jnp.exp2((m_run - m_new) * c_exp)
        p = jnp.exp2((s - m_new) * c_exp)
        l_new = alpha * l_run + jnp.sum(p, axis=1, keepdims=True)
        acc = alpha * acc + jnp.dot(p.astype(BF16), latent_piece(u, slot),
                                    preferred_element_type=F32)
        return m_new, l_new, acc

    def body(c, streams):
        g = b * n_chunks + c
        slot = g % n_slots
        wait_chunk(slot)
        start_chunk(jnp.minimum(g + ahead, last_g), (g + ahead) % n_slots)
        streams = list(streams)
        s_all = [scores(u, slot) for u in range(cp // piece)]
        for u, s in enumerate(s_all):
            streams[u % N_STREAMS] = absorb(streams[u % N_STREAMS], s, u, slot)
        return tuple(streams)

    empty = (jnp.full_like(m0, MASK_VALUE), jnp.zeros_like(l0), jnp.zeros_like(acc0))
    streams = lax.fori_loop(0, n_chunks, body, ((m0, l0, acc0),) + (empty,) * (N_STREAMS - 1))

    @pl.when(b == n_b - 1)
    def _():
        for a in range(1, n_slots):
            wait_chunk((last_g + a) % n_slots)

    m_fin = functools.reduce(jnp.maximum, [st[0] for st in streams])
    weights = [jnp.exp2((st[0] - m_fin) * c_exp) for st in streams]
    l_fin = sum(wt * st[1] for wt, st in zip(weights, streams))
    acc = sum(wt * st[2] for wt, st in zip(weights, streams))
    o_lat = (acc / l_fin).astype(BF16)
    full = jnp.dot(o_lat, wuv_ref[...], preferred_element_type=F32)
    r_id = lax.broadcasted_iota(jnp.int32, full.shape, 0)
    c_id = lax.broadcasted_iota(jnp.int32, full.shape, 1)
    full = jnp.where(c_id // v_dim == r_id % n_heads, full, 0.0)
    o_ref[0] = jnp.sum(full.reshape(rows // n_heads, n_heads, full.shape[1]), axis=1)


def _gelu(x):
    return 0.5 * x * (1.0 + lax.erf(x * (1.0 / math.sqrt(2.0))))


def _gmlp_uv(x, g_mix, w_in, g_v):
    a = _rms(x, g_mix).astype(BF16)
    z = _gelu(jnp.dot(a, w_in, preferred_element_type=F32))
    width = z.shape[1] // 2
    return z[:, :width], _rms(z[:, width:], g_v)


def _gmlp_prompt_kernel(x_ref, gmix_ref, win_ref, gv_ref, ws_ref, bias_ref, m_ref, *, chunk):
    u, v = _gmlp_uv(x_ref[...], gmix_ref[...], win_ref[...], gv_ref[...])
    vb = v.astype(BF16)
    n_groups = ws_ref.shape[0]
    r_id = lax.broadcasted_iota(jnp.int32, (chunk, chunk), 0)
    c_id = lax.broadcasted_iota(jnp.int32, (chunk, chunk), 1)
    tril = c_id <= r_id
    bias = bias_ref[...]
    for g in range(n_groups):
        wg = jnp.where(tril, ws_ref[g], 0.0).astype(BF16)
        gs = slice(g * LANES, (g + 1) * LANES)
        for c in range(x_ref.shape[0] // chunk):
            rs = slice(c * chunk, (c + 1) * chunk)
            s = jnp.dot(wg, vb[rs, gs], preferred_element_type=F32) + bias[:, gs]
            m_ref[rs, gs] = (u[rs, gs] * s).astype(m_ref.dtype)


def _gmlp_sample_kernel(x_ref, gmix_ref, win_ref, gv_ref, ws_ref, bias_ref, m_ref, v_ref, *, n_t):
    u, v = _gmlp_uv(x_ref[...], gmix_ref[...], win_ref[...], gv_ref[...])
    v_ref[...] = v
    db = x_ref.shape[0] // n_t
    vb = v.astype(BF16).astype(F32)
    for t in range(n_t):
        s = bias_ref[t:t + 1, :]
        for k in range(t + 1):
            w = ws_ref[t, k:k + 1, :].astype(BF16).astype(F32)
            s = s + w * vb[k * db:(k + 1) * db, :]
        m_ref[t * db:(t + 1) * db, :] = (u[t * db:(t + 1) * db, :] * s).astype(m_ref.dtype)


def _post_kernel(*refs, n_mix, ff_chunk, final):
    h_ref = refs[0]
    mix_refs = refs[1:1 + n_mix]
    wout_refs = refs[1 + n_mix:1 + 2 * n_mix]
    (p_ref, gffn_ref, w1_ref, w2_ref, gple_ref, wg_ref, wple_ref, gf_ref, out_ref) = refs[1 + 2 * n_mix:]
    h = h_ref[...]
    for m_ref, w_ref in zip(mix_refs, wout_refs):
        h = h + jnp.dot(m_ref[...], w_ref[...], preferred_element_type=F32)
    a = _rms(h, gffn_ref[...]).astype(BF16)
    d_ff = w1_ref.shape[1]
    y = None
    for c in range(d_ff // ff_chunk):
        cs = slice(c * ff_chunk, (c + 1) * ff_chunk)
        hid = jnp.maximum(jnp.dot(a, w1_ref[:, cs], preferred_element_type=F32), 0.0)
        part = jnp.dot((hid * hid).astype(BF16), w2_ref[cs, :], preferred_element_type=F32)
        y = part if y is None else y + part
    h = h + y
    a = _rms(h, gple_ref[...]).astype(BF16)
    gate = jax.nn.sigmoid(jnp.dot(a, wg_ref[...], preferred_element_type=F32))
    pe = jnp.dot(p_ref[...].astype(BF16), wple_ref[...], preferred_element_type=F32)
    h = h + gate * pe
    if final:
        h = _rms(h, gf_ref[...])
    out_ref[...] = h


def _row_spec(tm, width):
    return pl.BlockSpec((tm, width), lambda i: (i, 0))


def _ab_common_specs(tm, d, tab_width, tab_map, w):
    return [
        _row_spec(tm, d),
        pl.BlockSpec((tm, tab_width), tab_map),
        _const_spec((1, d)),
        _const_spec(w['w_in'].shape),
        _const_spec(w['g_q'].shape),
        _const_spec(w['g_kv'].shape),
        _const_spec(w['conv_w'].shape),
    ]


def _ab_prompt(h, tab, tab_t, w, batch, seq, tm):
    n, d = h.shape
    tiles_per_seq = seq // tm
    conv_w = w['conv_w'].shape[1]
    kv_rank = w['g_kv'].shape[1]
    rope = w['rope_dim']
    qw = w['w_qt'].shape[0] // 2
    vw = w['w_vt'].shape[0]
    in_specs = _ab_common_specs(tm, d, LANES, lambda i: (i % tiles_per_seq, 0), w) + [
        pl.BlockSpec((2 * LANES, tm), lambda i: (0, i % tiles_per_seq)),
        _const_spec(w['w_qt'].shape), _const_spec(w['w_k'].shape), _const_spec(w['e_k'].shape),
        _const_spec(w['w_vt'].shape)]
    out_shape = (
        jax.ShapeDtypeStruct((n, conv_w), BF16),
        jax.ShapeDtypeStruct((n, kv_rank), F32),
        jax.ShapeDtypeStruct((n, rope), F32),
        jax.ShapeDtypeStruct((batch, qw, seq), BF16),
        jax.ShapeDtypeStruct((n, qw), BF16),
        jax.ShapeDtypeStruct((batch, vw, seq), BF16),
        jax.ShapeDtypeStruct((batch, SUBLANES, conv_w), F32),
    )

    def feature_major(width):
        return pl.BlockSpec((1, width, tm), lambda i: (i // tiles_per_seq, 0, i % tiles_per_seq))

    out_specs = (
        _row_spec(tm, conv_w), _row_spec(tm, kv_rank), _row_spec(tm, rope),
        feature_major(qw), _row_spec(tm, qw), feature_major(vw),
        pl.BlockSpec((1, SUBLANES, conv_w), lambda i: (i // tiles_per_seq, 0, 0)),
    )
    return pl.pallas_call(
        functools.partial(_ab_prompt_kernel, tiles_per_seq=tiles_per_seq),
        grid=(n // tm,), in_specs=in_specs, out_specs=out_specs, out_shape=out_shape,
        scratch_shapes=[pltpu.VMEM((SUBLANES, conv_w), F32)],
        compiler_params=_params(("arbitrary",)), name="ab_prompt",
    )(h, tab, w['g_mix'], w['w_in'], w['g_q'], w['g_kv'], w['conv_w'],
      tab_t, w['w_qt'], w['w_k'], w['e_k'], w['w_vt'])


def _ab_sample(h, tab, w, state, n_t):
    n, d = h.shape
    db = n // n_t
    conv_w = w['conv_w'].shape[1]
    kv_rank = w['g_kv'].shape[1]
    rope = w['rope_dim']
    qw = w['w_q'].shape[1] // 2
    n_heads = w['w_ukt'].shape[0]
    in_specs = _ab_common_specs(db, d, 3 * LANES, lambda i: (i, 0), w) + [
        _const_spec(w['w_q'].shape), _const_spec(w['w_ukt'].shape), _const_spec(state.shape)]
    out_shape = (
        jax.ShapeDtypeStruct((n, conv_w), BF16),
        jax.ShapeDtypeStruct((n, kv_rank), F32),
        jax.ShapeDtypeStruct((n, rope), F32),
        jax.ShapeDtypeStruct((n, qw), BF16),
        jax.ShapeDtypeStruct((n_heads, n, kv_rank), BF16),
        jax.ShapeDtypeStruct((n, conv_w), F32),
    )
    out_specs = (
        _row_spec(db, conv_w), _row_spec(db, kv_rank), _row_spec(db, rope), _row_spec(db, qw),
        pl.BlockSpec((n_heads, db, kv_rank), lambda i: (0, i, 0)),
        _row_spec(db, conv_w),
    )
    return pl.pallas_call(
        _ab_sample_kernel,
        grid=(n_t,), in_specs=in_specs, out_specs=out_specs, out_shape=out_shape,
        scratch_shapes=[pltpu.VMEM((db, conv_w), F32), pltpu.VMEM((db, conv_w), F32)],
        compiler_params=_params(("arbitrary",)), name="ab_sample",
    )(h, tab, w['g_mix'], w['w_in'], w['g_q'], w['g_kv'], w['conv_w'],
      w['w_q'], w['w_ukt'], state)


def _flash(qt, k, vt, scale, n_heads, tq):
    batch, seq, qw = k.shape
    vw = vt.shape[1]
    v_dim = vw // n_heads
    nq = seq // tq
    pairs = [(i, j) for i in range(nq) for j in range(i + 1)]
    qi = jnp.asarray([p[0] for p in pairs], jnp.int32)
    kj = jnp.asarray([p[1] for p in pairs], jnp.int32)
    grid_spec = pltpu.PrefetchScalarGridSpec(
        num_scalar_prefetch=2,
        grid=(batch, len(pairs)),
        in_specs=[
            pl.BlockSpec((1, qw, tq), lambda b, t, qi, kj: (b, 0, qi[t])),
            pl.BlockSpec((1, tq, qw), lambda b, t, qi, kj: (b, kj[t], 0)),
            pl.BlockSpec((1, vw, tq), lambda b, t, qi, kj: (b, 0, kj[t])),
        ],
        out_specs=pl.BlockSpec((1, tq, vw), lambda b, t, qi, kj: (b, qi[t], 0)),
        scratch_shapes=[pltpu.VMEM((n_heads, 1, tq), F32),
                        pltpu.VMEM((n_heads, v_dim + SUM_ROWS, tq), F32)],
    )
    return pl.pallas_call(
        functools.partial(_flash_kernel, c_exp=scale * LOG2_E, n_heads=n_heads, v_dim=v_dim),
        grid_spec=grid_spec,
        out_shape=jax.ShapeDtypeStruct((batch, seq, vw), BF16),
        compiler_params=_params(("arbitrary", "arbitrary")), name="flash_prompt",
    )(qi, kj, qt, k, vt)


def _paged(page_table, ql, qp, kn, pn, w_uv, pool_c, pool_pt, scale, n_heads):
    db, rows, kv_rank = ql.shape
    rope = qp.shape[2]
    n_pages = page_table.shape[1]
    page = pool_c.shape[1]
    vw = w_uv.shape[1]
    n_t = rows // n_heads
    cp = min(PAGES_PER_CHUNK, n_pages)
    piece = min(PAGES_PER_PIECE, cp)
    assert n_pages % cp == 0 and cp % piece == 0
    grid_spec = pltpu.PrefetchScalarGridSpec(
        num_scalar_prefetch=1,
        grid=(db,),
        in_specs=[
            pl.BlockSpec((1, rows, kv_rank), lambda b, pt: (b, 0, 0)),
            pl.BlockSpec((1, rows, rope), lambda b, pt: (b, 0, 0)),
            pl.BlockSpec((1, SUBLANES, kv_rank), lambda b, pt: (b, 0, 0)),
            pl.BlockSpec((1, SUBLANES, rope), lambda b, pt: (b, 0, 0)),
            pl.BlockSpec(w_uv.shape, lambda b, pt: (0, 0)),
            pl.BlockSpec(memory_space=pl.ANY),
            pl.BlockSpec(memory_space=pl.ANY),
        ],
        out_specs=pl.BlockSpec((1, n_t, vw), lambda b, pt: (b, 0, 0)),
        scratch_shapes=[
            pltpu.VMEM((N_SLOTS, cp, page, kv_rank), F32),
            pltpu.VMEM((N_SLOTS, rope, cp * page), F32),
            pltpu.SemaphoreType.DMA((N_SLOTS,)),
            pltpu.SemaphoreType.DMA((N_SLOTS,)),
        ],
    )
    return pl.pallas_call(
        functools.partial(_paged_kernel, c_exp=scale * LOG2_E, n_pages=n_pages, n_heads=n_heads,
                          v_dim=vw // n_heads, piece=piece),
        grid_spec=grid_spec,
        out_shape=jax.ShapeDtypeStruct((db, n_t, vw), F32),
        compiler_params=_params(("arbitrary",)), name="paged_attn",
    )(page_table.reshape(-1), ql, qp, kn, pn, w_uv, pool_c, pool_pt)


def _gmlp_prompt(h, w, tm, chunk):
    n, d = h.shape
    width = w['w_in_c'].shape[1] // 2
    return pl.pallas_call(
        functools.partial(_gmlp_prompt_kernel, chunk=chunk),
        grid=(n // tm,),
        in_specs=[_row_spec(tm, d), _const_spec((1, d)), _const_spec(w['w_in_c'].shape),
                  _const_spec((1, width)), _const_spec(w['w_s'].shape),
                  _const_spec(w['bias_full'].shape)],
        out_specs=_row_spec(tm, width),
        out_shape=jax.ShapeDtypeStruct((n, width), BF16),
        compiler_params=_params(("arbitrary",)), name="gmlp_prompt",
    )(h, w['g_mix'], w['w_in_c'], w['g_v'], w['w_s'], w['bias_full'])


def _gmlp_sample(h, w, n_t):
    n, d = h.shape
    width = w['w_in_c'].shape[1] // 2
    return pl.pallas_call(
        functools.partial(_gmlp_sample_kernel, n_t=n_t),
        grid=(1,),
        in_specs=[_const_spec((n, d)), _const_spec((1, d)), _const_spec(w['w_in_c'].shape),
                  _const_spec((1, width)), _const_spec(w['ws_small'].shape),
                  _const_spec(w['bias_small'].shape)],
        out_specs=(_const_spec((n, width)), _const_spec((n, width))),
        out_shape=(jax.ShapeDtypeStruct((n, width), BF16), jax.ShapeDtypeStruct((n, width), F32)),
        compiler_params=_params(("arbitrary",)), name="gmlp_sample",
    )(h, w['g_mix'], w['w_in_c'], w['g_v'], w['ws_small'], w['bias_small'])


def _post(h, mixes, w_outs, p, w, tm, final, name):
    n, d = h.shape
    n_mix = len(mixes)
    ff_chunk = min(1024, w['w1'].shape[1])
    in_specs = ([_row_spec(tm, d)] + [_row_spec(tm, m.shape[1]) for m in mixes]
                + [_const_spec(wo.shape) for wo in w_outs]
                + [_row_spec(tm, p.shape[1]), _const_spec((1, d)), _const_spec(w['w1'].shape),
                   _const_spec(w['w2'].shape), _const_spec((1, d)), _const_spec(w['w_gate'].shape),
                   _const_spec(w['w_ple'].shape), _const_spec((1, d))])
    return pl.pallas_call(
        functools.partial(_post_kernel, n_mix=n_mix, ff_chunk=ff_chunk, final=final),
        grid=(n // tm,), in_specs=in_specs, out_specs=_row_spec(tm, d),
        out_shape=jax.ShapeDtypeStruct((n, d), F32),
        compiler_params=_params(("arbitrary",)), name=name,
    )(h, *mixes, *w_outs, p, w['g_ffn'], w['w1'], w['w2'], w['g_ple'], w['w_gate'], w['w_ple'],
      w['g_f'])


def _rope_table(pos, rope_dim):
    inv = ROPE_THETA ** (-jnp.arange(0, rope_dim, 2, dtype=F32) / rope_dim)
    ang = pos.astype(F32)[:, None] * inv[None, :]
    c, s = jnp.cos(ang), jnp.sin(ang)
    t = pos.shape[0]
    cc = jnp.concatenate([c, c], axis=1)
    ss = jnp.concatenate([-s, s], axis=1)
    zeros = lambda w_: jnp.zeros((t, w_), F32)
    nope = LANES - 2 * rope_dim
    key_tab = jnp.concatenate([cc, ss, zeros(LANES - 2 * rope_dim)], axis=1)
    q_cos = jnp.concatenate([jnp.ones((t, nope), F32), cc, zeros(LANES - nope - rope_dim)], axis=1)
    q_sin = jnp.concatenate([zeros(nope), ss, zeros(LANES - nope - rope_dim)], axis=1)
    return jnp.concatenate([key_tab, q_cos, q_sin], axis=1)


def _prep_layer_ab(j, norm_mix_i, w_in_ab, conv_w, q_norm, w_uq, kv_norm, w_uk, w_uv):
    d = w_in_ab.shape[1]
    kv_rank, n_heads, nope = w_uk.shape[1:]
    rope = w_uq.shape[2] // n_heads - nope
    v_dim = w_uv.shape[3]
    half = rope // 2
    in_ab = w_in_ab.shape[2]
    w_in = w_in_ab[j]
    kpe0 = in_ab - rope
    pad = (-(in_ab + rope)) % (2 * LANES)
    w_in_p = jnp.concatenate(
        [w_in, w_in[:, kpe0 + half:], w_in[:, kpe0:kpe0 + half], jnp.zeros((d, pad), F32)], axis=1)
    q_rank = w_uq.shape[1]
    uq = w_uq[j].reshape(q_rank, n_heads, nope + rope)
    zpad = jnp.zeros((q_rank, n_heads, LANES - nope - rope), F32)
    wq_a = jnp.concatenate([uq, zpad], axis=2)
    wq_b = jnp.concatenate([jnp.zeros((q_rank, n_heads, nope), F32), uq[:, :, nope + half:],
                            uq[:, :, nope:nope + half], zpad], axis=2)
    w_q = jnp.concatenate([wq_a.reshape(q_rank, -1), wq_b.reshape(q_rank, -1)], axis=1)
    uk = w_uk[j]
    w_k = jnp.concatenate([uk, jnp.zeros((kv_rank, n_heads, LANES - nope), F32)], axis=2)
    e_k = jnp.zeros((LANES, n_heads, LANES), F32)
    idx = jnp.arange(rope)
    e_k = e_k.at[idx, :, nope + idx].set(1.0)
    w_ukt = jnp.concatenate([uk.transpose(1, 2, 0),
                             jnp.zeros((n_heads, LANES - nope, kv_rank), F32)], axis=1)
    return dict(
        g_mix=norm_mix_i.reshape(1, -1), w_in=w_in_p.astype(BF16), g_q=q_norm[j].reshape(1, -1),
        w_q=w_q.astype(BF16), w_qt=w_q.T.astype(BF16), g_kv=kv_norm[j].reshape(1, -1),
        conv_w=conv_w[j],
        w_k=w_k.reshape(kv_rank, -1).astype(BF16), e_k=e_k.reshape(LANES, -1).astype(BF16),
        w_v=w_uv[j].reshape(kv_rank, n_heads * v_dim).astype(BF16),
        w_vt=w_uv[j].reshape(kv_rank, n_heads * v_dim).T.astype(BF16), w_ukt=w_ukt.astype(BF16),
        rope_dim=rope, n_heads=n_heads, nope=nope)


def _prep_layer_c(j, norm_mix_i, w_in_c, v_norm, w_s, b_s, n_t):
    n_groups, chunk = w_s.shape[1], w_s.shape[2]
    width = w_in_c.shape[2] // 2
    gw = width // n_groups
    bias_full = jnp.repeat(b_s[j].T, gw, axis=1)
    ws_small = jnp.repeat(w_s[j][:, :n_t, :n_t].transpose(1, 2, 0), gw, axis=2)
    return dict(g_mix=norm_mix_i.reshape(1, -1), w_in_c=w_in_c[j].astype(BF16),
                g_v=v_norm[j].reshape(1, -1), w_s=w_s[j], bias_full=bias_full,
                ws_small=ws_small, bias_small=bias_full[:n_t], chunk=chunk)


def _prep_post(i, norm_ffn, w_ff1, w_ff2, norm_ple, w_ple_gate, w_ple, norm_f):
    return dict(g_ffn=norm_ffn[i].reshape(1, -1), w1=w_ff1[i].astype(BF16), w2=w_ff2[i].astype(BF16),
                g_ple=norm_ple[i].reshape(1, -1), w_gate=w_ple_gate[i].astype(BF16),
                w_ple=w_ple[i].astype(BF16), g_f=norm_f.reshape(1, -1))


def _token_tile(n, cap):
    tm = min(cap, n)
    while n % tm:
        tm //= 2
    return tm


def kernel(x_prompt, x_sample, cache_ckv, cache_kpe, state_conv, page_table, p_prompt, p_sample, norm_mix, w_in_ab, conv_w, q_norm, w_uq, kv_norm, w_uk, w_uv, w_out_ab, w_in_c, v_norm, w_s, b_s, w_out_c, norm_ffn, w_ff1, w_ff2, norm_ple, w_ple_gate, w_ple, norm_f):
    depth = norm_mix.shape[0]
    batch, seq, d = x_prompt.shape
    db, n_t, _ = x_sample.shape
    n_pages = page_table.shape[1]
    page = cache_ckv.shape[2]
    past_len = n_pages * page
    n_heads, nope = w_uk.shape[2], w_uk.shape[3]
    rope = cache_kpe.shape[3]
    conv_width = conv_w.shape[2]
    scale = float(nope + rope) ** -0.5

    tab_p = _rope_table(jnp.arange(seq, dtype=jnp.int32), rope)
    tab_s = jnp.repeat(_rope_table(past_len + jnp.arange(n_t, dtype=jnp.int32), rope), db, axis=0)

    n_p = batch * seq
    n_s = db * n_t
    tm_p = _token_tile(seq, 512)
    tq = _token_tile(seq, 512)
    hp = x_prompt.reshape(n_p, d)
    hs = x_sample.transpose(1, 0, 2).reshape(n_s, d)
    pp = p_prompt.reshape(depth, n_p, -1)
    ps = p_sample.transpose(0, 2, 1, 3).reshape(depth, n_s, -1)

    def unmajor(a):
        return a.reshape(n_t, db, a.shape[-1]).transpose(1, 0, 2)

    conv_p, conv_s, ckv_p, kpe_p, ckv_s, kpe_s, v_s = [], [], [], [], [], [], []
    for i in range(depth):
        j = i // 2
        wpost = _prep_post(i, norm_ffn, w_ff1, w_ff2, norm_ple, w_ple_gate, w_ple, norm_f)
        final = i == depth - 1
        if i % 2 == 0:
            w = _prep_layer_ab(j, norm_mix[i], w_in_ab, conv_w, q_norm, w_uq, kv_norm, w_uk, w_uv)
            w_out = w_out_ab[j].astype(BF16)
            w_outs = [w_out[:conv_width], w_out[conv_width:]]
            yconv, ckv, kpe, qt, k, vt, cstate = _ab_prompt(
                hp, tab_p[:, :LANES], tab_p[:, LANES:].T, w, batch, seq, tm_p)
            yatt = _flash(qt, k.reshape(batch, seq, -1), vt, scale, n_heads, tq)
            hp = _post(hp, [yconv, yatt.reshape(n_p, -1)], w_outs, pp[i], wpost, tm_p, final,
                       "post_prompt")
            conv_p.append(cstate[:, SUBLANES - 2:, :])
            ckv_p.append(ckv.reshape(batch, seq, -1))
            kpe_p.append(kpe.reshape(batch, seq, -1))
            state = state_conv[j].transpose(1, 0, 2)
            yconv, ckv, kpe, q, qlat, up = _ab_sample(hs, tab_s, w, state, n_t)
            kv_rank = ckv.shape[1]
            ql = qlat.reshape(n_heads, n_t, db, kv_rank).transpose(2, 1, 0, 3).reshape(
                db, n_t * n_heads, kv_rank)
            qp = q.reshape(n_t, db, n_heads, LANES)[..., nope:nope + rope].transpose(
                1, 0, 2, 3).reshape(db, n_t * n_heads, rope)
            ckv_b = unmajor(ckv)
            kpe_b = unmajor(kpe)
            padr = ((0, 0), (0, SUBLANES - n_t), (0, 0))
            yatt = _paged(page_table, ql, qp, jnp.pad(ckv_b, padr), jnp.pad(kpe_b, padr),
                          w['w_v'], cache_ckv[j], jnp.swapaxes(cache_kpe[j], 1, 2), scale, n_heads)
            yatt = yatt.transpose(1, 0, 2).reshape(n_s, -1).astype(BF16)
            hs = _post(hs, [yconv, yatt], w_outs, ps[i], wpost, n_s, final, "post_sample")
            conv_s.append(unmajor(up)[:, n_t - 2:, :])
            ckv_s.append(ckv_b)
            kpe_s.append(kpe_b)
        else:
            w = _prep_layer_c(j, norm_mix[i], w_in_c, v_norm, w_s, b_s, n_t)
            w_outs = [w_out_c[j].astype(BF16)]
            m = _gmlp_prompt(hp, w, tm_p, w['chunk'])
            hp = _post(hp, [m], w_outs, pp[i], wpost, tm_p, final, "post_prompt")
            m, v = _gmlp_sample(hs, w, n_t)
            hs = _post(hs, [m], w_outs, ps[i], wpost, n_s, final, "post_sample")
            v_s.append(unmajor(v))

    y_prompt = hp.reshape(batch, seq, d)
    y_sample = unmajor(hs)
    return (y_prompt, y_sample, jnp.stack(conv_p), jnp.stack(conv_s), jnp.stack(ckv_p),
            jnp.stack(kpe_p), jnp.stack(ckv_s), jnp.stack(kpe_s), jnp.stack(v_s))
```

```python
import functools
import math

import jax
import jax.numpy as jnp
from jax import lax
from jax.experimental import pallas as pl
from jax.experimental.pallas import tpu as pltpu

F32 = jnp.float32
BF16 = jnp.bfloat16

EPS = 1e-6
ROPE_THETA = 10000.0
LANES = 128
SUBLANES = 8
VMEM_LIMIT_BYTES = 56 * 1024 * 1024
PAGES_PER_CHUNK = 32
PAGES_PER_PIECE = 8
N_STREAMS = 2
N_SLOTS = 4
FF_CHUNK = 1024
QK_LOOKAHEAD = 3
SUM_ROWS = 16
MASK_VALUE = -1e30
LOG2_E = math.log2(math.e)


def _rms(x, g):
    ms = jnp.mean(x * x, axis=-1, keepdims=True)
    return x * lax.rsqrt(ms + EPS) * g


def _const_spec(shape):
    nd = len(shape)
    return pl.BlockSpec(shape, lambda *_: (0,) * nd, pipeline_mode=pl.Buffered(1))


def _params(semantics):
    return pltpu.CompilerParams(dimension_semantics=semantics,
                                vmem_limit_bytes=VMEM_LIMIT_BYTES)


def _ab_project(x_ref, tab_ref, gmix_ref, win_ref, gq_ref, gkv_ref, cw_ref, rope):
    cw = cw_ref.shape[1]
    q_rank = gq_ref.shape[1]
    kv_rank = gkv_ref.shape[1]
    a = _rms(x_ref[...], gmix_ref[...]).astype(BF16)
    z = jnp.dot(a, win_ref[...], preferred_element_type=F32)
    o_q = 3 * cw
    o_kv = o_q + q_rank
    o_pe = o_kv + kv_rank
    up = z[:, cw:2 * cw] * z[:, 0:cw]
    gb = z[:, 2 * cw:o_q]
    ckv_n = _rms(z[:, o_kv:o_pe], gkv_ref[...])
    prod = z[:, o_pe:o_pe + LANES] * tab_ref[:, 0:LANES]
    kr = prod + pltpu.roll(prod, LANES - rope, axis=1)
    qn = _rms(z[:, o_q:o_kv], gq_ref[...]).astype(BF16)
    return up, gb, ckv_n, kr, qn


NT_DIMS = (((1,), (1,)), ((), ()))


def _ab_prompt_kernel(x_ref, tab_ref, gmix_ref, win_ref, gq_ref, gkv_ref, cw_ref,
                      tabt_ref, wqt_ref, wk_ref, wvt_ref,
                      yconv_ref, ckv_ref, kpe_ref, qt_ref, k_ref, vt_ref, cstate_ref,
                      carry_ref, *, tiles_per_seq, nope):
    i = pl.program_id(0)

    @pl.when(i % tiles_per_seq == 0)
    def _():
        carry_ref[...] = jnp.zeros_like(carry_ref)

    up, gb, ckv_n, kr, qn = _ab_project(
        x_ref, tab_ref, gmix_ref, win_ref, gq_ref, gkv_ref, cw_ref, kpe_ref.shape[1])
    tm = up.shape[0]
    c0 = carry_ref[SUBLANES - 2:SUBLANES - 1, :]
    c1 = carry_ref[SUBLANES - 1:SUBLANES, :]
    row = lax.broadcasted_iota(jnp.int32, (tm, 1), 0)
    um1 = jnp.where(row == 0, c1, pltpu.roll(up, 1, axis=0))
    um2 = jnp.where(row == 0, c0, jnp.where(row == 1, c1, pltpu.roll(up, 2, axis=0)))
    cw = cw_ref[...]
    conv = cw[0:1, :] * um2 + cw[1:2, :] * um1 + cw[2:3, :] * up
    yconv_ref[...] = (gb * conv).astype(yconv_ref.dtype)
    tail = up[tm - SUBLANES:, :]
    carry_ref[...] = tail
    cstate_ref[0] = tail

    ckv_ref[...] = ckv_n
    kpe_ref[...] = kr[:, 0:kpe_ref.shape[1]]
    rope = kpe_ref.shape[1]
    hr = rope // 2
    qt = lax.dot_general(wqt_ref[...], qn, NT_DIMS, preferred_element_type=F32)
    cos_t = tabt_ref[0:hr, :]
    sin_t = tabt_ref[hr:rope, :]
    pieces = []
    for h in range(qt.shape[0] // LANES):
        base = h * LANES
        x1 = qt[base + nope:base + nope + hr]
        x2 = qt[base + nope + hr:base + nope + rope]
        pieces += [qt[base:base + nope], x1 * cos_t - x2 * sin_t, x2 * cos_t + x1 * sin_t,
                   qt[base + nope + rope:base + LANES]]
    qt_ref[0] = jnp.concatenate(pieces, axis=0).astype(qt_ref.dtype)
    ckv_b = ckv_n.astype(BF16)
    lane = lax.broadcasted_iota(jnp.int32, kr.shape, 1)
    k_rot = jnp.where((lane >= nope) & (lane < nope + rope), pltpu.roll(kr, nope, axis=1), 0.0)
    k_cat = (jnp.dot(ckv_b, wk_ref[...], preferred_element_type=F32)
             + jnp.concatenate([k_rot] * (wk_ref.shape[1] // LANES), axis=1))
    k_ref[...] = k_cat.astype(k_ref.dtype)
    vt_ref[0] = lax.dot_general(wvt_ref[...], ckv_b, NT_DIMS,
                                preferred_element_type=F32).astype(vt_ref.dtype)


def _ab_sample_kernel(x_ref, tab_ref, gmix_ref, win_ref, gq_ref, gkv_ref, cw_ref,
                      wq_ref, wukt_ref, state_ref,
                      yconv_ref, ckv_ref, kpe_ref, q_ref, qlat_ref, up_ref,
                      prev1_ref, prev2_ref):
    t = pl.program_id(0)

    @pl.when(t == 0)
    def _():
        prev2_ref[...] = state_ref[0]
        prev1_ref[...] = state_ref[1]

    up, gb, ckv_n, kr, qn = _ab_project(
        x_ref, tab_ref, gmix_ref, win_ref, gq_ref, gkv_ref, cw_ref, kpe_ref.shape[1])
    q2 = jnp.dot(qn, wq_ref[...], preferred_element_type=F32)
    half = q2.shape[1] // 2
    cpat = jnp.concatenate([tab_ref[:, LANES:2 * LANES]] * (half // LANES), axis=1)
    spat = jnp.concatenate([tab_ref[:, 2 * LANES:3 * LANES]] * (half // LANES), axis=1)
    q_r = q2[:, :half] * cpat + q2[:, half:] * spat
    cw = cw_ref[...]
    prev1 = prev1_ref[...]
    conv = cw[0:1, :] * prev2_ref[...] + cw[1:2, :] * prev1 + cw[2:3, :] * up
    yconv_ref[...] = (gb * conv).astype(yconv_ref.dtype)
    up_ref[...] = up
    prev2_ref[...] = prev1
    prev1_ref[...] = up

    ckv_ref[...] = ckv_n
    kpe_ref[...] = kr[:, 0:kpe_ref.shape[1]]
    q_b = q_r.astype(BF16)
    q_ref[...] = q_b
    for h in range(qlat_ref.shape[0]):
        qlat_ref[h] = jnp.dot(q_b[:, h * LANES:(h + 1) * LANES], wukt_ref[h],
                              preferred_element_type=F32).astype(qlat_ref.dtype)


def _flash_kernel(qi_ref, kj_ref, qt_ref, k_ref, vt_ref, o_ref, m_ref, acc_ref, *,
                  c_exp, n_heads, v_dim):
    t = pl.program_id(1)
    qi = qi_ref[t]
    kj = kj_ref[t]
    tk = k_ref.shape[1]
    tq = qt_ref.shape[2]

    @pl.when(kj == 0)
    def _():
        m_ref[...] = jnp.full_like(m_ref, MASK_VALUE)
        acc_ref[...] = jnp.zeros_like(acc_ref)

    def step(masked):
        ones = jnp.ones((SUM_ROWS, tk), BF16)
        if masked:
            k_id = lax.broadcasted_iota(jnp.int32, (tk, tq), 0)
            q_id = lax.broadcasted_iota(jnp.int32, (tk, tq), 1)
            keep = k_id <= q_id
        def scores_t(h):
            k = k_ref[0, :, h * LANES:(h + 1) * LANES]
            qt = qt_ref[0, h * LANES:(h + 1) * LANES, :]
            return jnp.dot(k, qt, preferred_element_type=F32)

        pending = [scores_t(h) for h in range(min(QK_LOOKAHEAD, n_heads))]
        for h in range(n_heads):
            st = pending.pop(0)
            if h + QK_LOOKAHEAD < n_heads:
                pending.append(scores_t(h + QK_LOOKAHEAD))
            if masked:
                st = jnp.where(keep, st, MASK_VALUE)
            m_old = m_ref[h]
            m_new = jnp.maximum(m_old, jnp.max(st, axis=0, keepdims=True))
            alpha = jnp.exp2((m_old - m_new) * c_exp)
            pt = jnp.exp2((st - m_new) * c_exp).astype(BF16)
            vx = jnp.concatenate([vt_ref[0, h * v_dim:(h + 1) * v_dim, :], ones], axis=0)
            acc_ref[h] = alpha * acc_ref[h] + jnp.dot(vx, pt, preferred_element_type=F32)
            m_ref[h] = m_new

    @pl.when(kj < qi)
    def _():
        step(False)

    @pl.when(kj == qi)
    def _():
        step(True)
        ot = jnp.concatenate(
            [acc_ref[h, 0:v_dim, :] / acc_ref[h, v_dim:v_dim + 1, :] for h in range(n_heads)],
            axis=0)
        o_ref[0] = ot.T.astype(o_ref.dtype)


def _paged_kernel(pt_ref, ql_ref, qp_ref, kn_ref, pn_ref, wuv_ref, pool_c, pool_pt,
                  o_ref, cbuf, pbuf, sem_c, sem_p, *, c_exp, n_pages, n_heads, v_dim, piece):
    b = pl.program_id(0)
    n_b = pl.num_programs(0)
    cp = cbuf.shape[1]
    page = cbuf.shape[2]
    n_chunks = n_pages // cp

    def page_copies(pg, k, slot):
        return (pltpu.make_async_copy(pool_c.at[pg], cbuf.at[slot, k], sem_c.at[slot]),
                pltpu.make_async_copy(pool_pt.at[pg], pbuf.at[slot, :, pl.ds(k * page, page)],
                                      sem_p.at[slot]))

    def start_chunk(g, slot):
        for k in range(cp):
            for cpy in page_copies(pt_ref[g * cp + k], k, slot):
                cpy.start()

    def wait_chunk(slot):
        for k in range(cp):
            for cpy in page_copies(0, k, slot):
                cpy.wait()

    last_g = n_b * n_chunks - 1
    n_slots = cbuf.shape[0]
    ahead = n_slots - 1

    @pl.when(b == 0)
    def _():
        for a in range(ahead):
            start_chunk(jnp.minimum(a, last_g), a)

    ql = ql_ref[0]
    qp = qp_ref[0]
    rows = ql.shape[0]

    kn = kn_ref[0].astype(BF16)
    pn = pn_ref[0].astype(BF16)
    s0 = (lax.dot_general(ql, kn, NT_DIMS, preferred_element_type=F32)
          + lax.dot_general(qp, pn, NT_DIMS, preferred_element_type=F32))
    r_id = lax.broadcasted_iota(jnp.int32, s0.shape, 0)
    c_id = lax.broadcasted_iota(jnp.int32, s0.shape, 1)
    s0 = jnp.where(c_id <= r_id // n_heads, s0, MASK_VALUE)
    m0 = jnp.max(s0, axis=1, keepdims=True)
    p0 = jnp.exp2((s0 - m0) * c_exp)
    l0 = jnp.sum(p0, axis=1, keepdims=True)
    acc0 = jnp.dot(p0.astype(BF16), kn, preferred_element_type=F32)

    def latent_piece(u, slot):
        kc = cbuf[slot, u * piece:(u + 1) * piece].reshape(piece * page, cbuf.shape[3])
        return kc.astype(BF16)

    def scores(u, slot):
        kpt = pbuf[slot, :, u * piece * page:(u + 1) * piece * page].astype(BF16)
        return (lax.dot_general(ql, latent_piece(u, slot), NT_DIMS, preferred_element_type=F32)
                + jnp.dot(qp, kpt, preferred_element_type=F32))

    def absorb(state, s, u, slot):
        m_run, l_run, acc = state
        m_new = jnp.maximum(m_run, jnp.max(s, axis=1, keepdims=True))
        alpha = jnp.exp2((m_run - m_new) * c_exp)
        p = jnp.exp2((s - m_new) * c_exp)
        l_new = alpha * l_run + jnp.sum(p, axis=1, keepdims=True)
        acc = alpha * acc + jnp.dot(p.astype(BF16), latent_piece(u, slot),
                                    preferred_element_type=F32)
        return m_new, l_new, acc

    def body(c, streams):
        g = b * n_chunks + c
        slot = g % n_slots
        wait_chunk(slot)
        start_chunk(jnp.minimum(g + ahead, last_g), (g + ahead) % n_slots)
        streams = list(streams)
        s_all = [scores(u, slot) for u in range(cp // piece)]
        for u, s in enumerate(s_all):
            streams[u % N_STREAMS] = absorb(streams[u % N_STREAMS], s, u, slot)
        return tuple(streams)

    empty = (jnp.full_like(m0, MASK_VALUE), jnp.zeros_like(l0), jnp.zeros_like(acc0))
    streams = lax.fori_loop(0, n_chunks, body, ((m0, l0, acc0),) + (empty,) * (N_STREAMS - 1))

    @pl.when(b == n_b - 1)
    def _():
        for a in range(1, n_slots):
            wait_chunk((last_g + a) % n_slots)

    m_fin = functools.reduce(jnp.maximum, [st[0] for st in streams])
    weights = [jnp.exp2((st[0] - m_fin) * c_exp) for st in streams]
    l_fin = sum(wt * st[1] for wt, st in zip(weights, streams))
    acc = sum(wt * st[2] for wt, st in zip(weights, streams))
    o_lat = (acc / l_fin).astype(BF16)
    full = jnp.dot(o_lat, wuv_ref[...], preferred_element_type=F32)
    r_id = lax.broadcasted_iota(jnp.int32, full.shape, 0)
    c_id = lax.broadcasted_iota(jnp.int32, full.shape, 1)
    full = jnp.where(c_id // v_dim == r_id % n_heads, full, 0.0)
    o_ref[0] = jnp.sum(full.reshape(rows // n_heads, n_heads, full.shape[1]), axis=1)


def _gelu(x):
    return 0.5 * x * (1.0 + lax.erf(x * (1.0 / math.sqrt(2.0))))


def _gmlp_uv(x, g_mix, w_in, g_v):
    a = _rms(x, g_mix).astype(BF16)
    z = _gelu(jnp.dot(a, w_in, preferred_element_type=F32))
    width = z.shape[1] // 2
    return z[:, :width], _rms(z[:, width:], g_v)


def _gmlp_prompt_kernel(x_ref, gmix_ref, win_ref, gv_ref, ws_ref, bias_ref, m_ref, *, chunk):
    u, v = _gmlp_uv(x_ref[...], gmix_ref[...], win_ref[...], gv_ref[...])
    vb = v.astype(BF16)
    n_groups = ws_ref.shape[0]
    r_id = lax.broadcasted_iota(jnp.int32, (chunk, chunk), 0)
    c_id = lax.broadcasted_iota(jnp.int32, (chunk, chunk), 1)
    tril = c_id <= r_id
    bias = bias_ref[...]
    gw = m_ref.shape[1] // n_groups
    n_chunks = x_ref.shape[0] // chunk
    for g in range(n_groups):
        wg = jnp.where(tril, ws_ref[g], 0.0).astype(BF16)
        gs = slice(g * gw, (g + 1) * gw)
        rhs = jnp.concatenate([vb[c * chunk:(c + 1) * chunk, gs] for c in range(n_chunks)], axis=1)
        s_all = jnp.dot(wg, rhs, preferred_element_type=F32)
        for c in range(n_chunks):
            rs = slice(c * chunk, (c + 1) * chunk)
            s = s_all[:, c * gw:(c + 1) * gw] + bias[:, gs]
            m_ref[rs, gs] = (u[rs, gs] * s).astype(m_ref.dtype)


def _gmlp_sample_kernel(x_ref, gmix_ref, win_ref, gv_ref, ws_ref, bias_ref, m_ref, v_ref, *, n_t):
    u, v = _gmlp_uv(x_ref[...], gmix_ref[...], win_ref[...], gv_ref[...])
    v_ref[...] = v
    db = x_ref.shape[0] // n_t
    vb = v.astype(BF16).astype(F32)
    for t in range(n_t):
        s = bias_ref[t:t + 1, :]
        for k in range(t + 1):
            w = ws_ref[t, k:k + 1, :].astype(BF16).astype(F32)
            s = s + w * vb[k * db:(k + 1) * db, :]
        m_ref[t * db:(t + 1) * db, :] = (u[t * db:(t + 1) * db, :] * s).astype(m_ref.dtype)


def _post_kernel(*refs, n_mix, ff_chunk, final):
    h_ref = refs[0]
    mix_refs = refs[1:1 + n_mix]
    (wout_ref, p_ref, gffn_ref, w1_ref, w2_ref, gple_ref, wg_ref, wple_ref, gf_ref,
     out_ref) = refs[1 + n_mix:]
    h = h_ref[...]
    row = 0
    for m_ref in mix_refs:
        h = h + jnp.dot(m_ref[...], wout_ref[row:row + m_ref.shape[1], :],
                        preferred_element_type=F32)
        row += m_ref.shape[1]
    a = _rms(h, gffn_ref[...]).astype(BF16)
    d_ff = w1_ref.shape[1]
    y = None
    for c in range(d_ff // ff_chunk):
        cs = slice(c * ff_chunk, (c + 1) * ff_chunk)
        hid = jnp.maximum(jnp.dot(a, w1_ref[:, cs], preferred_element_type=F32), 0.0)
        part = jnp.dot((hid * hid).astype(BF16), w2_ref[cs, :], preferred_element_type=F32)
        y = part if y is None else y + part
    h = h + y
    a = _rms(h, gple_ref[...]).astype(BF16)
    gate = jax.nn.sigmoid(jnp.dot(a, wg_ref[...], preferred_element_type=F32))
    pe = jnp.dot(p_ref[...].astype(BF16), wple_ref[...], preferred_element_type=F32)
    h = h + gate * pe
    if final:
        h = _rms(h, gf_ref[...])
    out_ref[...] = h


def _row_spec(tm, width):
    return pl.BlockSpec((tm, width), lambda i, *_: (i, 0))


def _ab_common_specs(tm, d, tab_width, tab_map, w):
    return [
        _row_spec(tm, d),
        pl.BlockSpec((tm, tab_width), tab_map),
        _const_spec((1, d)),
        _const_spec(w['w_in'].shape),
        _const_spec(w['g_q'].shape),
        _const_spec(w['g_kv'].shape),
        _const_spec(w['conv_w'].shape),
    ]


def _ab_prompt(h, tab, tab_t, w, batch, seq, tm):
    n, d = h.shape
    tiles_per_seq = seq // tm
    conv_w = w['conv_w'].shape[1]
    kv_rank = w['g_kv'].shape[1]
    rope = w['rope_dim']
    qw = w['w_qt'].shape[0]
    vw = w['w_vt'].shape[0]
    in_specs = _ab_common_specs(tm, d, LANES, lambda i: (i % tiles_per_seq, 0), w) + [
        pl.BlockSpec((rope, tm), lambda i: (0, i % tiles_per_seq)),
        _const_spec(w['w_qt'].shape), _const_spec(w['w_k'].shape), _const_spec(w['w_vt'].shape)]
    out_shape = (
        jax.ShapeDtypeStruct((n, conv_w), BF16),
        jax.ShapeDtypeStruct((n, kv_rank), F32),
        jax.ShapeDtypeStruct((n, rope), F32),
        jax.ShapeDtypeStruct((batch, qw, seq), BF16),
        jax.ShapeDtypeStruct((n, qw), BF16),
        jax.ShapeDtypeStruct((batch, vw, seq), BF16),
        jax.ShapeDtypeStruct((batch, SUBLANES, conv_w), F32),
    )

    def feature_major(width):
        return pl.BlockSpec((1, width, tm), lambda i: (i // tiles_per_seq, 0, i % tiles_per_seq))

    out_specs = (
        _row_spec(tm, conv_w), _row_spec(tm, kv_rank), _row_spec(tm, rope),
        feature_major(qw), _row_spec(tm, qw), feature_major(vw),
        pl.BlockSpec((1, SUBLANES, conv_w), lambda i: (i // tiles_per_seq, 0, 0)),
    )
    return pl.pallas_call(
        functools.partial(_ab_prompt_kernel, tiles_per_seq=tiles_per_seq, nope=w['nope']),
        grid=(n // tm,), in_specs=in_specs, out_specs=out_specs, out_shape=out_shape,
        scratch_shapes=[pltpu.VMEM((SUBLANES, conv_w), F32)],
        compiler_params=_params(("arbitrary",)), name="ab_prompt",
    )(h, tab, w['g_mix'], w['w_in'], w['g_q'], w['g_kv'], w['conv_w'],
      tab_t, w['w_qt'], w['w_k'], w['w_vt'])


def _ab_sample(h, tab, w, state, n_t):
    n, d = h.shape
    db = n // n_t
    conv_w = w['conv_w'].shape[1]
    kv_rank = w['g_kv'].shape[1]
    rope = w['rope_dim']
    qw = w['w_q'].shape[1] // 2
    n_heads = w['w_ukt'].shape[0]
    in_specs = _ab_common_specs(db, d, 3 * LANES, lambda i: (i, 0), w) + [
        _const_spec(w['w_q'].shape), _const_spec(w['w_ukt'].shape), _const_spec(state.shape)]
    out_shape = (
        jax.ShapeDtypeStruct((n, conv_w), BF16),
        jax.ShapeDtypeStruct((n, kv_rank), F32),
        jax.ShapeDtypeStruct((n, rope), F32),
        jax.ShapeDtypeStruct((n, qw), BF16),
        jax.ShapeDtypeStruct((n_heads, n, kv_rank), BF16),
        jax.ShapeDtypeStruct((n, conv_w), F32),
    )
    out_specs = (
        _row_spec(db, conv_w), _row_spec(db, kv_rank), _row_spec(db, rope), _row_spec(db, qw),
        pl.BlockSpec((n_heads, db, kv_rank), lambda i: (0, i, 0)),
        _row_spec(db, conv_w),
    )
    return pl.pallas_call(
        _ab_sample_kernel,
        grid=(n_t,), in_specs=in_specs, out_specs=out_specs, out_shape=out_shape,
        scratch_shapes=[pltpu.VMEM((db, conv_w), F32), pltpu.VMEM((db, conv_w), F32)],
        compiler_params=_params(("arbitrary",)), name="ab_sample",
    )(h, tab, w['g_mix'], w['w_in'], w['g_q'], w['g_kv'], w['conv_w'],
      w['w_q'], w['w_ukt'], state)


def _flash(qt, k, vt, scale, n_heads, tq):
    batch, seq, qw = k.shape
    vw = vt.shape[1]
    v_dim = vw // n_heads
    nq = seq // tq
    pairs = [(i, j) for i in range(nq) for j in range(i + 1)]
    qi = jnp.asarray([p[0] for p in pairs], jnp.int32)
    kj = jnp.asarray([p[1] for p in pairs], jnp.int32)
    grid_spec = pltpu.PrefetchScalarGridSpec(
        num_scalar_prefetch=2,
        grid=(batch, len(pairs)),
        in_specs=[
            pl.BlockSpec((1, qw, tq), lambda b, t, qi, kj: (b, 0, qi[t])),
            pl.BlockSpec((1, tq, qw), lambda b, t, qi, kj: (b, kj[t], 0)),
            pl.BlockSpec((1, vw, tq), lambda b, t, qi, kj: (b, 0, kj[t])),
        ],
        out_specs=pl.BlockSpec((1, tq, vw), lambda b, t, qi, kj: (b, qi[t], 0)),
        scratch_shapes=[pltpu.VMEM((n_heads, 1, tq), F32),
                        pltpu.VMEM((n_heads, v_dim + SUM_ROWS, tq), F32)],
    )
    return pl.pallas_call(
        functools.partial(_flash_kernel, c_exp=scale * LOG2_E, n_heads=n_heads, v_dim=v_dim),
        grid_spec=grid_spec,
        out_shape=jax.ShapeDtypeStruct((batch, seq, vw), BF16),
        compiler_params=_params(("arbitrary", "arbitrary")), name="flash_prompt",
    )(qi, kj, qt, k, vt)


def _paged(page_table, ql, qp, kn, pn, w_uv, pool_c, pool_pt, scale, n_heads):
    db, rows, kv_rank = ql.shape
    rope = qp.shape[2]
    n_pages = page_table.shape[1]
    page = pool_c.shape[1]
    vw = w_uv.shape[1]
    n_t = rows // n_heads
    cp = min(PAGES_PER_CHUNK, n_pages)
    piece = min(PAGES_PER_PIECE, cp)
    assert n_pages % cp == 0 and cp % piece == 0

    def seq_block(r, c):
        return pl.BlockSpec((1, r, c), lambda b, pt: (b, 0, 0))

    grid_spec = pltpu.PrefetchScalarGridSpec(
        num_scalar_prefetch=1, grid=(db,),
        in_specs=[seq_block(rows, kv_rank), seq_block(rows, rope), seq_block(SUBLANES, kv_rank),
                  seq_block(SUBLANES, rope), _const_spec(w_uv.shape),
                  pl.BlockSpec(memory_space=pl.ANY), pl.BlockSpec(memory_space=pl.ANY)],
        out_specs=seq_block(n_t, vw),
        scratch_shapes=[pltpu.VMEM((N_SLOTS, cp, page, kv_rank), F32),
                        pltpu.VMEM((N_SLOTS, rope, cp * page), F32),
                        pltpu.SemaphoreType.DMA((N_SLOTS,)),
                        pltpu.SemaphoreType.DMA((N_SLOTS,))])
    return pl.pallas_call(
        functools.partial(_paged_kernel, c_exp=scale * LOG2_E, n_pages=n_pages, n_heads=n_heads,
                          v_dim=vw // n_heads, piece=piece),
        grid_spec=grid_spec, out_shape=jax.ShapeDtypeStruct((db, n_t, vw), F32),
        compiler_params=_params(("arbitrary",)), name="paged_attn",
    )(page_table.reshape(-1), ql, qp, kn, pn, w_uv, pool_c, pool_pt)


def _gmlp_prompt(h, w, tm, chunk):
    n, d = h.shape
    width = w['w_in_c'].shape[1] // 2
    return pl.pallas_call(
        functools.partial(_gmlp_prompt_kernel, chunk=chunk),
        grid=(n // tm,),
        in_specs=[_row_spec(tm, d), _const_spec((1, d)), _const_spec(w['w_in_c'].shape),
                  _const_spec((1, width)), _const_spec(w['w_s'].shape),
                  _const_spec(w['bias_full'].shape)],
        out_specs=_row_spec(tm, width),
        out_shape=jax.ShapeDtypeStruct((n, width), BF16),
        compiler_params=_params(("arbitrary",)), name="gmlp_prompt",
    )(h, w['g_mix'], w['w_in_c'], w['g_v'], w['w_s'], w['bias_full'])


def _gmlp_sample(h, w, n_t):
    n, d = h.shape
    width = w['w_in_c'].shape[1] // 2
    return pl.pallas_call(
        functools.partial(_gmlp_sample_kernel, n_t=n_t),
        grid=(1,),
        in_specs=[_const_spec((n, d)), _const_spec((1, d)), _const_spec(w['w_in_c'].shape),
                  _const_spec((1, width)), _const_spec(w['ws_small'].shape),
                  _const_spec(w['bias_small'].shape)],
        out_specs=(_const_spec((n, width)), _const_spec((n, width))),
        out_shape=(jax.ShapeDtypeStruct((n, width), BF16), jax.ShapeDtypeStruct((n, width), F32)),
        compiler_params=_params(("arbitrary",)), name="gmlp_sample",
    )(h, w['g_mix'], w['w_in_c'], w['g_v'], w['ws_small'], w['bias_small'])


def _post(h, mixes, w_out, p, layer, w, tm, final, name):
    n, d = h.shape
    in_specs = ([_row_spec(tm, d)] + [_row_spec(tm, m.shape[1]) for m in mixes]
                + [_const_spec(w_out.shape),
                   pl.BlockSpec((None, tm, p.shape[2]), lambda i: (layer, i, 0)),
                   _const_spec((1, d)), _const_spec(w['w1'].shape),
                   _const_spec(w['w2'].shape), _const_spec((1, d)), _const_spec(w['w_gate'].shape),
                   _const_spec(w['w_ple'].shape), _const_spec((1, d))])
    return pl.pallas_call(
        functools.partial(_post_kernel, n_mix=len(mixes),
                          ff_chunk=min(FF_CHUNK, w['w1'].shape[1]), final=final),
        grid=(n // tm,), in_specs=in_specs, out_specs=_row_spec(tm, d),
        out_shape=jax.ShapeDtypeStruct((n, d), F32),
        compiler_params=_params(("arbitrary",)), name=name,
    )(h, *mixes, w_out, p, w['g_ffn'], w['w1'], w['w2'], w['g_ple'], w['w_gate'], w['w_ple'],
      w['g_f'])


def _rope_table(pos, rope_dim):
    inv = ROPE_THETA ** (-jnp.arange(0, rope_dim, 2, dtype=F32) / rope_dim)
    ang = pos.astype(F32)[:, None] * inv[None, :]
    c, s = jnp.cos(ang), jnp.sin(ang)
    t = pos.shape[0]
    cc = jnp.concatenate([c, c], axis=1)
    ss = jnp.concatenate([-s, s], axis=1)
    zeros = lambda w_: jnp.zeros((t, w_), F32)
    nope = LANES - 2 * rope_dim
    key_tab = jnp.concatenate([cc, ss, zeros(LANES - 2 * rope_dim)], axis=1)
    q_cos = jnp.concatenate([jnp.ones((t, nope), F32), cc, zeros(LANES - nope - rope_dim)], axis=1)
    q_sin = jnp.concatenate([zeros(nope), ss, zeros(LANES - nope - rope_dim)], axis=1)
    return jnp.concatenate([key_tab, q_cos, q_sin], axis=1)


def _prep_layer_ab(j, norm_mix_i, w_in_ab, conv_w, q_norm, w_uq, kv_norm, w_uk, w_uv):
    d = w_in_ab.shape[1]
    kv_rank, n_heads, nope = w_uk.shape[1:]
    rope = w_uq.shape[2] // n_heads - nope
    v_dim = w_uv.shape[3]
    half = rope // 2
    in_ab = w_in_ab.shape[2]
    w_in = w_in_ab[j]
    kpe0 = in_ab - rope
    pad = (-(in_ab + rope)) % (2 * LANES)
    w_in_p = jnp.concatenate(
        [w_in, w_in[:, kpe0 + half:], w_in[:, kpe0:kpe0 + half], jnp.zeros((d, pad), F32)], axis=1)
    q_rank = w_uq.shape[1]
    uq = w_uq[j].reshape(q_rank, n_heads, nope + rope)
    zpad = jnp.zeros((q_rank, n_heads, LANES - nope - rope), F32)
    wq_a = jnp.concatenate([uq, zpad], axis=2)
    wq_b = jnp.concatenate([jnp.zeros((q_rank, n_heads, nope), F32), uq[:, :, nope + half:],
                            uq[:, :, nope:nope + half], zpad], axis=2)
    w_q = jnp.concatenate([wq_a.reshape(q_rank, -1), wq_b.reshape(q_rank, -1)], axis=1)
    uk = w_uk[j]
    w_k = jnp.concatenate([uk, jnp.zeros((kv_rank, n_heads, LANES - nope), F32)], axis=2)
    w_ukt = jnp.concatenate([uk.transpose(1, 2, 0),
                             jnp.zeros((n_heads, LANES - nope, kv_rank), F32)], axis=1)
    return dict(
        g_mix=norm_mix_i.reshape(1, -1), w_in=w_in_p.astype(BF16), g_q=q_norm[j].reshape(1, -1),
        w_q=w_q.astype(BF16), w_qt=wq_a.reshape(q_rank, -1).T.astype(BF16),
        g_kv=kv_norm[j].reshape(1, -1), conv_w=conv_w[j],
        w_k=w_k.reshape(kv_rank, -1).astype(BF16),
        w_v=w_uv[j].reshape(kv_rank, n_heads * v_dim).astype(BF16),
        w_vt=w_uv[j].reshape(kv_rank, n_heads * v_dim).T.astype(BF16), w_ukt=w_ukt.astype(BF16),
        rope_dim=rope, n_heads=n_heads, nope=nope)


def _prep_layer_c(j, norm_mix_i, w_in_c, v_norm, w_s, b_s, n_t):
    n_groups, chunk = w_s.shape[1], w_s.shape[2]
    width = w_in_c.shape[2] // 2
    gw = width // n_groups
    bias_full = jnp.repeat(b_s[j].T, gw, axis=1)
    ws_small = jnp.repeat(w_s[j][:, :n_t, :n_t].transpose(1, 2, 0), gw, axis=2)
    return dict(g_mix=norm_mix_i.reshape(1, -1), w_in_c=w_in_c[j].astype(BF16),
                g_v=v_norm[j].reshape(1, -1), w_s=w_s[j], bias_full=bias_full,
                ws_small=ws_small, bias_small=bias_full[:n_t], chunk=chunk)


def _prep_post(i, norm_ffn, w_ff1, w_ff2, norm_ple, w_ple_gate, w_ple, norm_f):
    return dict(g_ffn=norm_ffn[i].reshape(1, -1), w1=w_ff1[i].astype(BF16), w2=w_ff2[i].astype(BF16),
                g_ple=norm_ple[i].reshape(1, -1), w_gate=w_ple_gate[i].astype(BF16),
                w_ple=w_ple[i].astype(BF16), g_f=norm_f.reshape(1, -1))


def _token_tile(n, cap):
    tm = min(cap, n)
    while n % tm:
        tm //= 2
    return tm


def kernel(x_prompt, x_sample, cache_ckv, cache_kpe, state_conv, page_table, p_prompt, p_sample, norm_mix, w_in_ab, conv_w, q_norm, w_uq, kv_norm, w_uk, w_uv, w_out_ab, w_in_c, v_norm, w_s, b_s, w_out_c, norm_ffn, w_ff1, w_ff2, norm_ple, w_ple_gate, w_ple, norm_f):
    depth = norm_mix.shape[0]
    batch, seq, d = x_prompt.shape
    db, n_t, _ = x_sample.shape
    n_pages = page_table.shape[1]
    page = cache_ckv.shape[2]
    past_len = n_pages * page
    n_heads, nope = w_uk.shape[2], w_uk.shape[3]
    rope = cache_kpe.shape[3]
    scale = float(nope + rope) ** -0.5

    tab_p = _rope_table(jnp.arange(seq, dtype=jnp.int32), rope)
    tab_s = jnp.repeat(_rope_table(past_len + jnp.arange(n_t, dtype=jnp.int32), rope), db, axis=0)

    n_p = batch * seq
    n_s = db * n_t
    tm_p = _token_tile(seq, 512)
    tq = _token_tile(seq, 512)
    hp = x_prompt.reshape(n_p, d)
    hs = x_sample.transpose(1, 0, 2).reshape(n_s, d)
    pp = p_prompt.reshape(depth, n_p, -1)
    ps = p_sample.transpose(0, 2, 1, 3).reshape(depth, n_s, -1)

    def unmajor(a):
        return a.reshape(n_t, db, a.shape[-1]).transpose(1, 0, 2)

    conv_p, conv_s, ckv_p, kpe_p, ckv_s, kpe_s, v_s = [], [], [], [], [], [], []
    for i in range(depth):
        j = i // 2
        wpost = _prep_post(i, norm_ffn, w_ff1, w_ff2, norm_ple, w_ple_gate, w_ple, norm_f)
        final = i == depth - 1
        if i % 2 == 0:
            w = _prep_layer_ab(j, norm_mix[i], w_in_ab, conv_w, q_norm, w_uq, kv_norm, w_uk, w_uv)
            w_out = w_out_ab[j].astype(BF16)
            hr = rope // 2
            cos_sin_t = jnp.concatenate([tab_p[:, :hr], tab_p[:, rope + hr:2 * rope]], axis=1).T
            yconv, ckv, kpe, qt, k, vt, cstate = _ab_prompt(
                hp, tab_p[:, :LANES], cos_sin_t, w, batch, seq, tm_p)
            yatt_p = _flash(qt, k.reshape(batch, seq, -1), vt, scale, n_heads, tq)
            conv_p.append(cstate[:, SUBLANES - 2:, :])
            ckv_p.append(ckv.reshape(batch, seq, -1))
            kpe_p.append(kpe.reshape(batch, seq, -1))
            state = state_conv[j].transpose(1, 0, 2)
            yconv_s, ckv, kpe, q, qlat, up = _ab_sample(hs, tab_s, w, state, n_t)
            kv_rank = ckv.shape[1]
            ql = qlat.reshape(n_heads, n_t, db, kv_rank).transpose(2, 1, 0, 3).reshape(
                db, n_t * n_heads, kv_rank)
            qp = q.reshape(n_t, db, n_heads, LANES)[..., nope:nope + rope].transpose(
                1, 0, 2, 3).reshape(db, n_t * n_heads, rope)
            ckv_b = unmajor(ckv)
            kpe_b = unmajor(kpe)
            padr = ((0, 0), (0, SUBLANES - n_t), (0, 0))
            hp = _post(hp, [yconv, yatt_p.reshape(n_p, -1)], w_out, pp, i, wpost, tm_p, final,
                       "post_prompt")
            yatt = _paged(page_table, ql, qp, jnp.pad(ckv_b, padr), jnp.pad(kpe_b, padr),
                          w['w_v'], cache_ckv[j], jnp.swapaxes(cache_kpe[j], 1, 2), scale, n_heads)
            yatt = yatt.transpose(1, 0, 2).reshape(n_s, -1).astype(BF16)
            hs = _post(hs, [yconv_s, yatt], w_out, ps, i, wpost, n_s, final, "post_sample")
            conv_s.append(unmajor(up)[:, n_t - 2:, :])
            ckv_s.append(ckv_b)
            kpe_s.append(kpe_b)
        else:
            w = _prep_layer_c(j, norm_mix[i], w_in_c, v_norm, w_s, b_s, n_t)
            w_out = w_out_c[j].astype(BF16)
            m = _gmlp_prompt(hp, w, tm_p, w['chunk'])
            hp = _post(hp, [m], w_out, pp, i, wpost, tm_p, final, "post_prompt")
            m, v = _gmlp_sample(hs, w, n_t)
            hs = _post(hs, [m], w_out, ps, i, wpost, n_s, final, "post_sample")
            v_s.append(unmajor(v))

    y_prompt = hp.reshape(batch, seq, d)
    y_sample = unmajor(hs)
    return (y_prompt, y_sample, jnp.stack(conv_p), jnp.stack(conv_s), jnp.stack(ckv_p),
            jnp.stack(kpe_p), jnp.stack(ckv_s), jnp.stack(kpe_s), jnp.stack(v_s))
```

```python
import functools
import math

import jax
import jax.numpy as jnp
from jax import lax
from jax.experimental import pallas as pl
from jax.experimental.pallas import tpu as pltpu

F32 = jnp.float32
BF16 = jnp.bfloat16

EPS = 1e-6
ROPE_THETA = 10000.0
LANES = 128
SUBLANES = 8
VMEM_LIMIT_BYTES = 62 * 1024 * 1024
PAGES_PER_CHUNK = 32
PAGES_PER_PIECE = 8
N_STREAMS = 2
N_SLOTS = 4
FF_CHUNK = 1024
POST_GROUP_ROWS = 512
POST_TILE_ROWS = 1024
QK_LOOKAHEAD = 3
SUM_ROWS = 16
MASK_VALUE = -1e30
LOG2_E = math.log2(math.e)


def _rms(x, g):
    ms = jnp.mean(x * x, axis=-1, keepdims=True)
    return x * lax.rsqrt(ms + EPS) * g


def _const_spec(shape):
    nd = len(shape)
    return pl.BlockSpec(shape, lambda *_: (0,) * nd, pipeline_mode=pl.Buffered(1))


def _params(semantics):
    return pltpu.CompilerParams(dimension_semantics=semantics,
                                vmem_limit_bytes=VMEM_LIMIT_BYTES)


def _ab_project(x_ref, tab_ref, gmix_ref, win_ref, gq_ref, gkv_ref, cw_ref, rope):
    cw = cw_ref.shape[1]
    q_rank = gq_ref.shape[1]
    kv_rank = gkv_ref.shape[1]
    a = _rms(x_ref[...], gmix_ref[...]).astype(BF16)
    z = jnp.dot(a, win_ref[...], preferred_element_type=F32)
    o_q = 3 * cw
    o_kv = o_q + q_rank
    o_pe = o_kv + kv_rank
    up = z[:, cw:2 * cw] * z[:, 0:cw]
    gb = z[:, 2 * cw:o_q]
    ckv_n = _rms(z[:, o_kv:o_pe], gkv_ref[...])
    prod = z[:, o_pe:o_pe + LANES] * tab_ref[:, 0:LANES]
    kr = prod + pltpu.roll(prod, LANES - rope, axis=1)
    qn = _rms(z[:, o_q:o_kv], gq_ref[...]).astype(BF16)
    return up, gb, ckv_n, kr, qn


NT_DIMS = (((1,), (1,)), ((), ()))


def _ab_prompt_kernel(x_ref, tab_ref, gmix_ref, win_ref, gq_ref, gkv_ref, cw_ref,
                      tabt_ref, wqt_ref, wk_ref, wvt_ref,
                      yconv_ref, ckv_ref, kpe_ref, qt_ref, k_ref, vt_ref, cstate_ref,
                      carry_ref, *, tiles_per_seq, nope):
    i = pl.program_id(0)

    @pl.when(i % tiles_per_seq == 0)
    def _():
        carry_ref[...] = jnp.zeros_like(carry_ref)

    up, gb, ckv_n, kr, qn = _ab_project(
        x_ref, tab_ref, gmix_ref, win_ref, gq_ref, gkv_ref, cw_ref, kpe_ref.shape[1])
    tm = up.shape[0]
    c0 = carry_ref[SUBLANES - 2:SUBLANES - 1, :]
    c1 = carry_ref[SUBLANES - 1:SUBLANES, :]
    row = lax.broadcasted_iota(jnp.int32, (tm, 1), 0)
    um1 = jnp.where(row == 0, c1, pltpu.roll(up, 1, axis=0))
    um2 = jnp.where(row == 0, c0, jnp.where(row == 1, c1, pltpu.roll(up, 2, axis=0)))
    cw = cw_ref[...]
    conv = cw[0:1, :] * um2 + cw[1:2, :] * um1 + cw[2:3, :] * up
    yconv_ref[...] = (gb * conv).astype(yconv_ref.dtype)
    tail = up[tm - SUBLANES:, :]
    carry_ref[...] = tail
    cstate_ref[0] = tail

    ckv_ref[...] = ckv_n
    kpe_ref[...] = kr[:, 0:kpe_ref.shape[1]]
    rope = kpe_ref.shape[1]
    hr = rope // 2
    qt = lax.dot_general(wqt_ref[...], qn, NT_DIMS, preferred_element_type=F32)
    cos_t = tabt_ref[0:hr, :]
    sin_t = tabt_ref[hr:rope, :]
    pieces = []
    for h in range(qt.shape[0] // LANES):
        base = h * LANES
        x1 = qt[base + nope:base + nope + hr]
        x2 = qt[base + nope + hr:base + nope + rope]
        pieces += [qt[base:base + nope], x1 * cos_t - x2 * sin_t, x2 * cos_t + x1 * sin_t,
                   qt[base + nope + rope:base + LANES]]
    qt_ref[0] = jnp.concatenate(pieces, axis=0).astype(qt_ref.dtype)
    ckv_b = ckv_n.astype(BF16)
    lane = lax.broadcasted_iota(jnp.int32, kr.shape, 1)
    k_rot = jnp.where((lane >= nope) & (lane < nope + rope), pltpu.roll(kr, nope, axis=1), 0.0)
    k_cat = (jnp.dot(ckv_b, wk_ref[...], preferred_element_type=F32)
             + jnp.concatenate([k_rot] * (wk_ref.shape[1] // LANES), axis=1))
    k_ref[...] = k_cat.astype(k_ref.dtype)
    vt_ref[0] = lax.dot_general(wvt_ref[...], ckv_b, NT_DIMS,
                                preferred_element_type=F32).astype(vt_ref.dtype)


def _ab_sample_kernel(x_ref, tab_ref, gmix_ref, win_ref, gq_ref, gkv_ref, cw_ref,
                      wq_ref, wukt_ref, state_ref,
                      yconv_ref, ckv_ref, kpe_ref, q_ref, qlat_ref, up_ref,
                      prev1_ref, prev2_ref):
    t = pl.program_id(0)

    @pl.when(t == 0)
    def _():
        prev2_ref[...] = state_ref[0]
        prev1_ref[...] = state_ref[1]

    up, gb, ckv_n, kr, qn = _ab_project(
        x_ref, tab_ref, gmix_ref, win_ref, gq_ref, gkv_ref, cw_ref, kpe_ref.shape[1])
    q2 = jnp.dot(qn, wq_ref[...], preferred_element_type=F32)
    half = q2.shape[1] // 2
    cpat = jnp.concatenate([tab_ref[:, LANES:2 * LANES]] * (half // LANES), axis=1)
    spat = jnp.concatenate([tab_ref[:, 2 * LANES:3 * LANES]] * (half // LANES), axis=1)
    q_r = q2[:, :half] * cpat + q2[:, half:] * spat
    cw = cw_ref[...]
    prev1 = prev1_ref[...]
    conv = cw[0:1, :] * prev2_ref[...] + cw[1:2, :] * prev1 + cw[2:3, :] * up
    yconv_ref[...] = (gb * conv).astype(yconv_ref.dtype)
    up_ref[...] = up
    prev2_ref[...] = prev1
    prev1_ref[...] = up

    ckv_ref[...] = ckv_n
    kpe_ref[...] = kr[:, 0:kpe_ref.shape[1]]
    q_b = q_r.astype(BF16)
    q_ref[...] = q_b
    for h in range(qlat_ref.shape[0]):
        qlat_ref[h] = jnp.dot(q_b[:, h * LANES:(h + 1) * LANES], wukt_ref[h],
                              preferred_element_type=F32).astype(qlat_ref.dtype)


def _flash_kernel(qi_ref, kj_ref, qt_ref, k_ref, vt_ref, o_ref, m_ref, acc_ref, *,
                  c_exp, n_heads, v_dim):
    t = pl.program_id(1)
    qi = qi_ref[t]
    kj = kj_ref[t]
    tk = k_ref.shape[1]
    tq = qt_ref.shape[2]

    @pl.when(kj == 0)
    def _():
        m_ref[...] = jnp.full_like(m_ref, MASK_VALUE)
        acc_ref[...] = jnp.zeros_like(acc_ref)

    def step(masked):
        ones = jnp.ones((SUM_ROWS, tk), BF16)
        if masked:
            k_id = lax.broadcasted_iota(jnp.int32, (tk, tq), 0)
            q_id = lax.broadcasted_iota(jnp.int32, (tk, tq), 1)
            keep = k_id <= q_id
        def scores_t(h):
            k = k_ref[0, :, h * LANES:(h + 1) * LANES]
            qt = qt_ref[0, h * LANES:(h + 1) * LANES, :]
            return jnp.dot(k, qt, preferred_element_type=F32)

        pending = [scores_t(h) for h in range(min(QK_LOOKAHEAD, n_heads))]
        for h in range(n_heads):
            st = pending.pop(0)
            if h + QK_LOOKAHEAD < n_heads:
                pending.append(scores_t(h + QK_LOOKAHEAD))
            if masked:
                st = jnp.where(keep, st, MASK_VALUE)
            m_old = m_ref[h]
            m_new = jnp.maximum(m_old, jnp.max(st, axis=0, keepdims=True))
            alpha = jnp.exp2((m_old - m_new) * c_exp)
            pt = jnp.exp2((st - m_new) * c_exp).astype(BF16)
            vx = jnp.concatenate([vt_ref[0, h * v_dim:(h + 1) * v_dim, :], ones], axis=0)
            acc_ref[h] = alpha * acc_ref[h] + jnp.dot(vx, pt, preferred_element_type=F32)
            m_ref[h] = m_new

    @pl.when(kj < qi)
    def _():
        step(False)

    @pl.when(kj == qi)
    def _():
        step(True)
        ot = jnp.concatenate(
            [acc_ref[h, 0:v_dim, :] / acc_ref[h, v_dim:v_dim + 1, :] for h in range(n_heads)],
            axis=0)
        o_ref[0] = ot.T.astype(o_ref.dtype)


def _paged_kernel(pt_ref, ql_ref, qp_ref, kn_ref, pn_ref, pool_c, pool_pt,
                  o_ref, cbuf, pbuf, sem_c, sem_p, *, c_exp, n_pages, n_heads, piece):
    b = pl.program_id(0)
    n_b = pl.num_programs(0)
    cp = cbuf.shape[1]
    page = cbuf.shape[2]
    n_chunks = n_pages // cp

    def page_copies(pg, k, slot):
        return (pltpu.make_async_copy(pool_c.at[pg], cbuf.at[slot, k], sem_c.at[slot]),
                pltpu.make_async_copy(pool_pt.at[pg], pbuf.at[slot, :, pl.ds(k * page, page)],
                                      sem_p.at[slot]))

    def start_chunk(g, slot):
        for k in range(cp):
            for cpy in page_copies(pt_ref[g * cp + k], k, slot):
                cpy.start()

    def wait_chunk(slot):
        for k in range(cp):
            for cpy in page_copies(0, k, slot):
                cpy.wait()

    last_g = n_b * n_chunks - 1
    n_slots = cbuf.shape[0]
    ahead = n_slots - 1

    @pl.when(b == 0)
    def _():
        for a in range(ahead):
            start_chunk(jnp.minimum(a, last_g), a)

    ql = ql_ref[0]
    qp = qp_ref[0]
    rows = ql.shape[0]

    kn = kn_ref[0].astype(BF16)
    pn = pn_ref[0].astype(BF16)
    s0 = (lax.dot_general(ql, kn, NT_DIMS, preferred_element_type=F32)
          + lax.dot_general(qp, pn, NT_DIMS, preferred_element_type=F32))
    r_id = lax.broadcasted_iota(jnp.int32, s0.shape, 0)
    c_id = lax.broadcasted_iota(jnp.int32, s0.shape, 1)
    s0 = jnp.where(c_id <= r_id // n_heads, s0, MASK_VALUE)
    m0 = jnp.max(s0, axis=1, keepdims=True)
    p0 = jnp.exp2((s0 - m0) * c_exp)
    l0 = jnp.sum(p0, axis=1, keepdims=True)
    acc0 = jnp.dot(p0.astype(BF16), kn, preferred_element_type=F32)

    def latent_piece(u, slot):
        kc = cbuf[slot, u * piece:(u + 1) * piece].reshape(piece * page, cbuf.shape[3])
        return kc.astype(BF16)

    def scores(u, slot):
        kpt = pbuf[slot, :, u * piece * page:(u + 1) * piece * page].astype(BF16)
        return (lax.dot_general(ql, latent_piece(u, slot), NT_DIMS, preferred_element_type=F32)
                + jnp.dot(qp, kpt, preferred_element_type=F32))

    def absorb(state, s, u, slot):
        m_run, l_run, acc = state
        m_new = jnp.maximum(m_run, jnp.max(s, axis=1, keepdims=True))
        alpha = jnp.exp2((m_run - m_new) * c_exp)
        p = jnp.exp2((s - m_new) * c_exp)
        l_new = alpha * l_run + jnp.sum(p, axis=1, keepdims=True)
        acc = alpha * acc + jnp.dot(p.astype(BF16), latent_piece(u, slot),
                                    preferred_element_type=F32)
        return m_new, l_new, acc

    def body(c, streams):
        g = b * n_chunks + c
        slot = g % n_slots
        wait_chunk(slot)
        start_chunk(jnp.minimum(g + ahead, last_g), (g + ahead) % n_slots)
        streams = list(streams)
        s_all = [scores(u, slot) for u in range(cp // piece)]
        for u, s in enumerate(s_all):
            streams[u % N_STREAMS] = absorb(streams[u % N_STREAMS], s, u, slot)
        return tuple(streams)

    empty = (jnp.full_like(m0, MASK_VALUE), jnp.zeros_like(l0), jnp.zeros_like(acc0))
    streams = lax.fori_loop(0, n_chunks, body, ((m0, l0, acc0),) + (empty,) * (N_STREAMS - 1))

    @pl.when(b == n_b - 1)
    def _():
        for a in range(1, n_slots):
            wait_chunk((last_g + a) % n_slots)

    m_fin = functools.reduce(jnp.maximum, [st[0] for st in streams])
    weights = [jnp.exp2((st[0] - m_fin) * c_exp) for st in streams]
    l_fin = sum(wt * st[1] for wt, st in zip(weights, streams))
    acc = sum(wt * st[2] for wt, st in zip(weights, streams))
    o_ref[0] = (acc / l_fin).astype(o_ref.dtype)


def _latent_out_kernel(o_ref, wuv_ref, y_ref, *, n_heads, v_dim):
    full = jnp.dot(o_ref[...], wuv_ref[...], preferred_element_type=F32)
    r_id = lax.broadcasted_iota(jnp.int32, full.shape, 0)
    c_id = lax.broadcasted_iota(jnp.int32, full.shape, 1)
    full = jnp.where(c_id // v_dim == r_id % n_heads, full, 0.0)
    y_ref[...] = jnp.sum(full.reshape(full.shape[0] // n_heads, n_heads, full.shape[1]),
                         axis=1).astype(y_ref.dtype)


def _gelu(x):
    return 0.5 * x * (1.0 + lax.erf(x * (1.0 / math.sqrt(2.0))))


def _gmlp_uv(x, g_mix, w_in, g_v):
    a = _rms(x, g_mix).astype(BF16)
    z = _gelu(jnp.dot(a, w_in, preferred_element_type=F32))
    width = z.shape[1] // 2
    return z[:, :width], _rms(z[:, width:], g_v)


def _gmlp_prompt_kernel(x_ref, gmix_ref, win_ref, gv_ref, ws_ref, bias_ref, m_ref, *, chunk,
                        n_sub):
    sub = x_ref.shape[0] // n_sub
    uvs = [_gmlp_uv(x_ref[r * sub:(r + 1) * sub, :], gmix_ref[...], win_ref[...], gv_ref[...])
           for r in range(n_sub)]
    n_groups = ws_ref.shape[0]
    r_id = lax.broadcasted_iota(jnp.int32, (chunk, chunk), 0)
    c_id = lax.broadcasted_iota(jnp.int32, (chunk, chunk), 1)
    tril = c_id <= r_id
    bias = bias_ref[...]
    gw = m_ref.shape[1] // n_groups
    n_chunks = sub // chunk
    w_tril = [jnp.where(tril, ws_ref[g], 0.0).astype(BF16) for g in range(n_groups)]
    for r, (u, v) in enumerate(uvs):
        vb = v.astype(BF16)
        for g in range(n_groups):
            gs = slice(g * gw, (g + 1) * gw)
            rhs = jnp.concatenate([vb[c * chunk:(c + 1) * chunk, gs] for c in range(n_chunks)],
                                  axis=1)
            s_all = jnp.dot(w_tril[g], rhs, preferred_element_type=F32)
            for c in range(n_chunks):
                rs = slice(c * chunk, (c + 1) * chunk)
                s = s_all[:, c * gw:(c + 1) * gw] + bias[:, gs]
                m_ref[r * sub + c * chunk:r * sub + (c + 1) * chunk, gs] = (
                    u[rs, gs] * s).astype(m_ref.dtype)


def _gmlp_sample_kernel(x_ref, gmix_ref, win_ref, gv_ref, ws_ref, bias_ref, m_ref, v_ref, *, n_t):
    u, v = _gmlp_uv(x_ref[...], gmix_ref[...], win_ref[...], gv_ref[...])
    v_ref[...] = v
    db = x_ref.shape[0] // n_t
    vb = v.astype(BF16).astype(F32)
    for t in range(n_t):
        s = bias_ref[t:t + 1, :]
        for k in range(t + 1):
            w = ws_ref[t, k:k + 1, :].astype(BF16).astype(F32)
            s = s + w * vb[k * db:(k + 1) * db, :]
        m_ref[t * db:(t + 1) * db, :] = (u[t * db:(t + 1) * db, :] * s).astype(m_ref.dtype)


def _post_kernel(*refs, n_mix, ff_chunk, final, n_sub):
    h_ref = refs[0]
    mix_refs = refs[1:1 + n_mix]
    (wout_ref, p_ref, gffn_ref, w1_ref, w2_ref, gple_ref, wg_ref, wple_ref, gf_ref,
     out_ref) = refs[1 + n_mix:]
    sub = h_ref.shape[0] // n_sub
    groups = [slice(r * sub, (r + 1) * sub) for r in range(n_sub)]

    hs = []
    for rs in groups:
        h = h_ref[rs, :]
        row = 0
        for m_ref in mix_refs:
            h = h + jnp.dot(m_ref[rs, :], wout_ref[row:row + m_ref.shape[1], :],
                            preferred_element_type=F32)
            row += m_ref.shape[1]
        hs.append(h)
    acts = [_rms(h, gffn_ref[...]).astype(BF16) for h in hs]
    ys = [None] * n_sub
    for c in range(w1_ref.shape[1] // ff_chunk):
        cs = slice(c * ff_chunk, (c + 1) * ff_chunk)
        for r in range(n_sub):
            hid = jnp.maximum(jnp.dot(acts[r], w1_ref[:, cs], preferred_element_type=F32), 0.0)
            part = jnp.dot((hid * hid).astype(BF16), w2_ref[cs, :], preferred_element_type=F32)
            ys[r] = part if ys[r] is None else ys[r] + part
    hs = [h + y for h, y in zip(hs, ys)]
    acts = [_rms(h, gple_ref[...]).astype(BF16) for h in hs]
    gates = [jax.nn.sigmoid(jnp.dot(a, wg_ref[...], preferred_element_type=F32)) for a in acts]
    for rs, h, gate in zip(groups, hs, gates):
        pe = jnp.dot(p_ref[rs, :].astype(BF16), wple_ref[...], preferred_element_type=F32)
        h = h + gate * pe
        if final:
            h = _rms(h, gf_ref[...])
        out_ref[rs, :] = h


def _row_spec(tm, width):
    return pl.BlockSpec((tm, width), lambda i, *_: (i, 0))


def _ab_common_specs(tm, d, tab_width, tab_map, w):
    return [
        _row_spec(tm, d),
        pl.BlockSpec((tm, tab_width), tab_map),
        _const_spec((1, d)),
        _const_spec(w['w_in'].shape),
        _const_spec(w['g_q'].shape),
        _const_spec(w['g_kv'].shape),
        _const_spec(w['conv_w'].shape),
    ]


def _ab_prompt(h, tab, tab_t, w, batch, seq, tm):
    n, d = h.shape
    tiles_per_seq = seq // tm
    conv_w = w['conv_w'].shape[1]
    kv_rank = w['g_kv'].shape[1]
    rope = w['rope_dim']
    qw = w['w_qt'].shape[0]
    vw = w['w_vt'].shape[0]
    in_specs = _ab_common_specs(tm, d, LANES, lambda i: (i % tiles_per_seq, 0), w) + [
        pl.BlockSpec((rope, tm), lambda i: (0, i % tiles_per_seq)),
        _const_spec(w['w_qt'].shape), _const_spec(w['w_k'].shape), _const_spec(w['w_vt'].shape)]
    out_shape = (
        jax.ShapeDtypeStruct((n, conv_w), BF16),
        jax.ShapeDtypeStruct((n, kv_rank), F32),
        jax.ShapeDtypeStruct((n, rope), F32),
        jax.ShapeDtypeStruct((batch, qw, seq), BF16),
        jax.ShapeDtypeStruct((n, qw), BF16),
        jax.ShapeDtypeStruct((batch, vw, seq), BF16),
        jax.ShapeDtypeStruct((batch, SUBLANES, conv_w), F32),
    )

    def feature_major(width):
        return pl.BlockSpec((1, width, tm), lambda i: (i // tiles_per_seq, 0, i % tiles_per_seq))

    out_specs = (
        _row_spec(tm, conv_w), _row_spec(tm, kv_rank), _row_spec(tm, rope),
        feature_major(qw), _row_spec(tm, qw), feature_major(vw),
        pl.BlockSpec((1, SUBLANES, conv_w), lambda i: (i // tiles_per_seq, 0, 0)),
    )
    return pl.pallas_call(
        functools.partial(_ab_prompt_kernel, tiles_per_seq=tiles_per_seq, nope=w['nope']),
        grid=(n // tm,), in_specs=in_specs, out_specs=out_specs, out_shape=out_shape,
        scratch_shapes=[pltpu.VMEM((SUBLANES, conv_w), F32)],
        compiler_params=_params(("arbitrary",)), name="ab_prompt",
    )(h, tab, w['g_mix'], w['w_in'], w['g_q'], w['g_kv'], w['conv_w'],
      tab_t, w['w_qt'], w['w_k'], w['w_vt'])


def _ab_sample(h, tab, w, state, n_t):
    n, d = h.shape
    db = n // n_t
    conv_w = w['conv_w'].shape[1]
    kv_rank = w['g_kv'].shape[1]
    rope = w['rope_dim']
    qw = w['w_q'].shape[1] // 2
    n_heads = w['w_ukt'].shape[0]
    in_specs = _ab_common_specs(db, d, 3 * LANES, lambda i: (i, 0), w) + [
        _const_spec(w['w_q'].shape), _const_spec(w['w_ukt'].shape), _const_spec(state.shape)]
    out_shape = (
        jax.ShapeDtypeStruct((n, conv_w), BF16),
        jax.ShapeDtypeStruct((n, kv_rank), F32),
        jax.ShapeDtypeStruct((n, rope), F32),
        jax.ShapeDtypeStruct((n, qw), BF16),
        jax.ShapeDtypeStruct((n_heads, n, kv_rank), BF16),
        jax.ShapeDtypeStruct((n, conv_w), F32),
    )
    out_specs = (
        _row_spec(db, conv_w), _row_spec(db, kv_rank), _row_spec(db, rope), _row_spec(db, qw),
        pl.BlockSpec((n_heads, db, kv_rank), lambda i: (0, i, 0)),
        _row_spec(db, conv_w),
    )
    return pl.pallas_call(
        _ab_sample_kernel,
        grid=(n_t,), in_specs=in_specs, out_specs=out_specs, out_shape=out_shape,
        scratch_shapes=[pltpu.VMEM((db, conv_w), F32), pltpu.VMEM((db, conv_w), F32)],
        compiler_params=_params(("arbitrary",)), name="ab_sample",
    )(h, tab, w['g_mix'], w['w_in'], w['g_q'], w['g_kv'], w['conv_w'],
      w['w_q'], w['w_ukt'], state)


def _flash(qt, k, vt, scale, n_heads, tq):
    batch, seq, qw = k.shape
    vw = vt.shape[1]
    v_dim = vw // n_heads
    nq = seq // tq
    pairs = [(i, j) for i in range(nq) for j in range(i + 1)]
    qi = jnp.asarray([p[0] for p in pairs], jnp.int32)
    kj = jnp.asarray([p[1] for p in pairs], jnp.int32)
    grid_spec = pltpu.PrefetchScalarGridSpec(
        num_scalar_prefetch=2,
        grid=(batch, len(pairs)),
        in_specs=[
            pl.BlockSpec((1, qw, tq), lambda b, t, qi, kj: (b, 0, qi[t])),
            pl.BlockSpec((1, tq, qw), lambda b, t, qi, kj: (b, kj[t], 0)),
            pl.BlockSpec((1, vw, tq), lambda b, t, qi, kj: (b, 0, kj[t])),
        ],
        out_specs=pl.BlockSpec((1, tq, vw), lambda b, t, qi, kj: (b, qi[t], 0)),
        scratch_shapes=[pltpu.VMEM((n_heads, 1, tq), F32),
                        pltpu.VMEM((n_heads, v_dim + SUM_ROWS, tq), F32)],
    )
    return pl.pallas_call(
        functools.partial(_flash_kernel, c_exp=scale * LOG2_E, n_heads=n_heads, v_dim=v_dim),
        grid_spec=grid_spec,
        out_shape=jax.ShapeDtypeStruct((batch, seq, vw), BF16),
        compiler_params=_params(("arbitrary", "arbitrary")), name="flash_prompt",
    )(qi, kj, qt, k, vt)


def _paged(page_table, ql, qp, kn, pn, w_uv, pool_c, pool_pt, scale, n_heads):
    db, rows, kv_rank = ql.shape
    rope = qp.shape[2]
    n_pages = page_table.shape[1]
    page = pool_c.shape[1]
    vw = w_uv.shape[1]
    cp = min(PAGES_PER_CHUNK, n_pages)
    piece = min(PAGES_PER_PIECE, cp)
    assert n_pages % cp == 0 and cp % piece == 0

    def seq_block(r, c):
        return pl.BlockSpec((1, r, c), lambda b, pt: (b, 0, 0))

    grid_spec = pltpu.PrefetchScalarGridSpec(
        num_scalar_prefetch=1, grid=(db,),
        in_specs=[seq_block(rows, kv_rank), seq_block(rows, rope), seq_block(SUBLANES, kv_rank),
                  seq_block(SUBLANES, rope),
                  pl.BlockSpec(memory_space=pl.ANY), pl.BlockSpec(memory_space=pl.ANY)],
        out_specs=seq_block(rows, kv_rank),
        scratch_shapes=[pltpu.VMEM((N_SLOTS, cp, page, kv_rank), F32),
                        pltpu.VMEM((N_SLOTS, rope, cp * page), F32),
                        pltpu.SemaphoreType.DMA((N_SLOTS,)),
                        pltpu.SemaphoreType.DMA((N_SLOTS,))])
    o_lat = pl.pallas_call(
        functools.partial(_paged_kernel, c_exp=scale * LOG2_E, n_pages=n_pages, n_heads=n_heads,
                          piece=piece),
        grid_spec=grid_spec, out_shape=jax.ShapeDtypeStruct((db, rows, kv_rank), BF16),
        compiler_params=_params(("arbitrary",)), name="paged_attn",
    )(page_table.reshape(-1), ql, qp, kn, pn, pool_c, pool_pt)
    n_tok = db * rows // n_heads
    return pl.pallas_call(
        functools.partial(_latent_out_kernel, n_heads=n_heads, v_dim=vw // n_heads),
        grid=(1,),
        in_specs=[_const_spec((db * rows, kv_rank)), _const_spec(w_uv.shape)],
        out_specs=_const_spec((n_tok, vw)),
        out_shape=jax.ShapeDtypeStruct((n_tok, vw), BF16),
        compiler_params=_params(("arbitrary",)), name="latent_out",
    )(o_lat.reshape(db * rows, kv_rank), w_uv)


def _gmlp_prompt(h, w, tm, chunk):
    n, d = h.shape
    width = w['w_in_c'].shape[1] // 2
    return pl.pallas_call(
        functools.partial(_gmlp_prompt_kernel, chunk=chunk, n_sub=max(1, tm // POST_GROUP_ROWS)),
        grid=(n // tm,),
        in_specs=[_row_spec(tm, d), _const_spec((1, d)), _const_spec(w['w_in_c'].shape),
                  _const_spec((1, width)), _const_spec(w['w_s'].shape),
                  _const_spec(w['bias_full'].shape)],
        out_specs=_row_spec(tm, width),
        out_shape=jax.ShapeDtypeStruct((n, width), BF16),
        compiler_params=_params(("arbitrary",)), name="gmlp_prompt",
    )(h, w['g_mix'], w['w_in_c'], w['g_v'], w['w_s'], w['bias_full'])


def _gmlp_sample(h, w, n_t):
    n, d = h.shape
    width = w['w_in_c'].shape[1] // 2
    return pl.pallas_call(
        functools.partial(_gmlp_sample_kernel, n_t=n_t),
        grid=(1,),
        in_specs=[_const_spec((n, d)), _const_spec((1, d)), _const_spec(w['w_in_c'].shape),
                  _const_spec((1, width)), _const_spec(w['ws_small'].shape),
                  _const_spec(w['bias_small'].shape)],
        out_specs=(_const_spec((n, width)), _const_spec((n, width))),
        out_shape=(jax.ShapeDtypeStruct((n, width), BF16), jax.ShapeDtypeStruct((n, width), F32)),
        compiler_params=_params(("arbitrary",)), name="gmlp_sample",
    )(h, w['g_mix'], w['w_in_c'], w['g_v'], w['ws_small'], w['bias_small'])


def _post(h, mixes, w_out, p, layer, w, tm, final, name):
    n, d = h.shape
    in_specs = ([_row_spec(tm, d)] + [_row_spec(tm, m.shape[1]) for m in mixes]
                + [_const_spec(w_out.shape),
                   pl.BlockSpec((None, tm, p.shape[2]), lambda i: (layer, i, 0)),
                   _const_spec((1, d)), _const_spec(w['w1'].shape),
                   _const_spec(w['w2'].shape), _const_spec((1, d)), _const_spec(w['w_gate'].shape),
                   _const_spec(w['w_ple'].shape), _const_spec((1, d))])
    return pl.pallas_call(
        functools.partial(_post_kernel, n_mix=len(mixes), ff_chunk=min(FF_CHUNK, w['w1'].shape[1]),
                          final=final, n_sub=max(1, tm // POST_GROUP_ROWS)),
        grid=(n // tm,), in_specs=in_specs, out_specs=_row_spec(tm, d),
        out_shape=jax.ShapeDtypeStruct((n, d), F32),
        compiler_params=_params(("arbitrary",)), name=name,
    )(h, *mixes, w_out, p, w['g_ffn'], w['w1'], w['w2'], w['g_ple'], w['w_gate'], w['w_ple'],
      w['g_f'])


def _rope_table(pos, rope_dim):
    inv = ROPE_THETA ** (-jnp.arange(0, rope_dim, 2, dtype=F32) / rope_dim)
    ang = pos.astype(F32)[:, None] * inv[None, :]
    c, s = jnp.cos(ang), jnp.sin(ang)
    t = pos.shape[0]
    cc = jnp.concatenate([c, c], axis=1)
    ss = jnp.concatenate([-s, s], axis=1)
    zeros = lambda w_: jnp.zeros((t, w_), F32)
    nope = LANES - 2 * rope_dim
    key_tab = jnp.concatenate([cc, ss, zeros(LANES - 2 * rope_dim)], axis=1)
    q_cos = jnp.concatenate([jnp.ones((t, nope), F32), cc, zeros(LANES - nope - rope_dim)], axis=1)
    q_sin = jnp.concatenate([zeros(nope), ss, zeros(LANES - nope - rope_dim)], axis=1)
    return jnp.concatenate([key_tab, q_cos, q_sin], axis=1)


def _prep_layer_ab(j, norm_mix_i, w_in_ab, conv_w, q_norm, w_uq, kv_norm, w_uk, w_uv):
    d = w_in_ab.shape[1]
    kv_rank, n_heads, nope = w_uk.shape[1:]
    rope = w_uq.shape[2] // n_heads - nope
    v_dim = w_uv.shape[3]
    half = rope // 2
    in_ab = w_in_ab.shape[2]
    w_in = w_in_ab[j]
    kpe0 = in_ab - rope
    pad = (-(in_ab + rope)) % (2 * LANES)
    w_in_p = jnp.concatenate(
        [w_in, w_in[:, kpe0 + half:], w_in[:, kpe0:kpe0 + half], jnp.zeros((d, pad), F32)], axis=1)
    q_rank = w_uq.shape[1]
    uq = w_uq[j].reshape(q_rank, n_heads, nope + rope)
    zpad = jnp.zeros((q_rank, n_heads, LANES - nope - rope), F32)
    wq_a = jnp.concatenate([uq, zpad], axis=2)
    wq_b = jnp.concatenate([jnp.zeros((q_rank, n_heads, nope), F32), uq[:, :, nope + half:],
                            uq[:, :, nope:nope + half], zpad], axis=2)
    w_q = jnp.concatenate([wq_a.reshape(q_rank, -1), wq_b.reshape(q_rank, -1)], axis=1)
    uk = w_uk[j]
    w_k = jnp.concatenate([uk, jnp.zeros((kv_rank, n_heads, LANES - nope), F32)], axis=2)
    w_ukt = jnp.concatenate([uk.transpose(1, 2, 0),
                             jnp.zeros((n_heads, LANES - nope, kv_rank), F32)], axis=1)
    return dict(
        g_mix=norm_mix_i.reshape(1, -1), w_in=w_in_p.astype(BF16), g_q=q_norm[j].reshape(1, -1),
        w_q=w_q.astype(BF16), w_qt=wq_a.reshape(q_rank, -1).T.astype(BF16),
        g_kv=kv_norm[j].reshape(1, -1), conv_w=conv_w[j],
        w_k=w_k.reshape(kv_rank, -1).astype(BF16),
        w_v=w_uv[j].reshape(kv_rank, n_heads * v_dim).astype(BF16),
        w_vt=w_uv[j].reshape(kv_rank, n_heads * v_dim).T.astype(BF16), w_ukt=w_ukt.astype(BF16),
        rope_dim=rope, n_heads=n_heads, nope=nope)


def _prep_layer_c(j, norm_mix_i, w_in_c, v_norm, w_s, b_s, n_t):
    n_groups, chunk = w_s.shape[1], w_s.shape[2]
    width = w_in_c.shape[2] // 2
    gw = width // n_groups
    bias_full = jnp.repeat(b_s[j].T, gw, axis=1)
    ws_small = jnp.repeat(w_s[j][:, :n_t, :n_t].transpose(1, 2, 0), gw, axis=2)
    return dict(g_mix=norm_mix_i.reshape(1, -1), w_in_c=w_in_c[j].astype(BF16),
                g_v=v_norm[j].reshape(1, -1), w_s=w_s[j], bias_full=bias_full,
                ws_small=ws_small, bias_small=bias_full[:n_t], chunk=chunk)


def _prep_post(i, norm_ffn, w_ff1, w_ff2, norm_ple, w_ple_gate, w_ple, norm_f):
    return dict(g_ffn=norm_ffn[i].reshape(1, -1), w1=w_ff1[i].astype(BF16), w2=w_ff2[i].astype(BF16),
                g_ple=norm_ple[i].reshape(1, -1), w_gate=w_ple_gate[i].astype(BF16),
                w_ple=w_ple[i].astype(BF16), g_f=norm_f.reshape(1, -1))


def _token_tile(n, cap):
    tm = min(cap, n)
    while n % tm:
        tm //= 2
    return tm


def kernel(x_prompt, x_sample, cache_ckv, cache_kpe, state_conv, page_table, p_prompt, p_sample, norm_mix, w_in_ab, conv_w, q_norm, w_uq, kv_norm, w_uk, w_uv, w_out_ab, w_in_c, v_norm, w_s, b_s, w_out_c, norm_ffn, w_ff1, w_ff2, norm_ple, w_ple_gate, w_ple, norm_f):
    depth = norm_mix.shape[0]
    batch, seq, d = x_prompt.shape
    db, n_t, _ = x_sample.shape
    n_pages = page_table.shape[1]
    page = cache_ckv.shape[2]
    past_len = n_pages * page
    n_heads, nope = w_uk.shape[2], w_uk.shape[3]
    rope = cache_kpe.shape[3]
    scale = float(nope + rope) ** -0.5

    tab_p = _rope_table(jnp.arange(seq, dtype=jnp.int32), rope)
    tab_s = jnp.repeat(_rope_table(past_len + jnp.arange(n_t, dtype=jnp.int32), rope), db, axis=0)

    n_p = batch * seq
    n_s = db * n_t
    tm_p = _token_tile(seq, 512)
    tm_post = _token_tile(seq, POST_TILE_ROWS)
    tq = _token_tile(seq, 512)
    hp = x_prompt.reshape(n_p, d)
    hs = x_sample.transpose(1, 0, 2).reshape(n_s, d)
    pp = p_prompt.reshape(depth, n_p, -1)
    ps = p_sample.transpose(0, 2, 1, 3).reshape(depth, n_s, -1)

    def unmajor(a):
        return a.reshape(n_t, db, a.shape[-1]).transpose(1, 0, 2)

    conv_p, conv_s, ckv_p, kpe_p, ckv_s, kpe_s, v_s = [], [], [], [], [], [], []
    for i in range(depth):
        j = i // 2
        wpost = _prep_post(i, norm_ffn, w_ff1, w_ff2, norm_ple, w_ple_gate, w_ple, norm_f)
        final = i == depth - 1
        if i % 2 == 0:
            w = _prep_layer_ab(j, norm_mix[i], w_in_ab, conv_w, q_norm, w_uq, kv_norm, w_uk, w_uv)
            w_out = w_out_ab[j].astype(BF16)
            hr = rope // 2
            cos_sin_t = jnp.concatenate([tab_p[:, :hr], tab_p[:, rope + hr:2 * rope]], axis=1).T
            yconv, ckv, kpe, qt, k, vt, cstate = _ab_prompt(
                hp, tab_p[:, :LANES], cos_sin_t, w, batch, seq, tm_p)
            yatt_p = _flash(qt, k.reshape(batch, seq, -1), vt, scale, n_heads, tq)
            conv_p.append(cstate[:, SUBLANES - 2:, :])
            ckv_p.append(ckv.reshape(batch, seq, -1))
            kpe_p.append(kpe.reshape(batch, seq, -1))
            state = state_conv[j].transpose(1, 0, 2)
            yconv_s, ckv, kpe, q, qlat, up = _ab_sample(hs, tab_s, w, state, n_t)
            kv_rank = ckv.shape[1]
            ql = qlat.reshape(n_heads, n_t, db, kv_rank).transpose(2, 1, 0, 3).reshape(
                db, n_t * n_heads, kv_rank)
            qp = q.reshape(n_t, db, n_heads, LANES)[..., nope:nope + rope].transpose(
                1, 0, 2, 3).reshape(db, n_t * n_heads, rope)
            ckv_b = unmajor(ckv)
            kpe_b = unmajor(kpe)
            padr = ((0, 0), (0, SUBLANES - n_t), (0, 0))
            hp = _post(hp, [yconv, yatt_p.reshape(n_p, -1)], w_out, pp, i, wpost, tm_post, final,
                       "post_prompt")
            yatt = _paged(page_table, ql, qp, jnp.pad(ckv_b, padr), jnp.pad(kpe_b, padr),
                          w['w_v'], cache_ckv[j], jnp.swapaxes(cache_kpe[j], 1, 2), scale, n_heads)
            yatt = yatt.reshape(db, n_t, -1).transpose(1, 0, 2).reshape(n_s, -1)
            hs = _post(hs, [yconv_s, yatt], w_out, ps, i, wpost, n_s, final, "post_sample")
            conv_s.append(unmajor(up)[:, n_t - 2:, :])
            ckv_s.append(ckv_b)
            kpe_s.append(kpe_b)
        else:
            w = _prep_layer_c(j, norm_mix[i], w_in_c, v_norm, w_s, b_s, n_t)
            w_out = w_out_c[j].astype(BF16)
            m = _gmlp_prompt(hp, w, tm_post, w['chunk'])
            hp = _post(hp, [m], w_out, pp, i, wpost, tm_post, final, "post_prompt")
            m, v = _gmlp_sample(hs, w, n_t)
            hs = _post(hs, [m], w_out, ps, i, wpost, n_s, final, "post_sample")
            v_s.append(unmajor(v))

    y_prompt = hp.reshape(batch, seq, d)
    y_sample = unmajor(hs)
    return (y_prompt, y_sample, jnp.stack(conv_p), jnp.stack(conv_s), jnp.stack(ckv_p),
            jnp.stack(kpe_p), jnp.stack(ckv_s), jnp.stack(kpe_s), jnp.stack(v_s))
```

```python
import functools
import math

import jax
import jax.numpy as jnp
from jax import lax
from jax.experimental import pallas as pl
from jax.experimental.pallas import tpu as pltpu

F32 = jnp.float32
BF16 = jnp.bfloat16

EPS = 1e-6
ROPE_THETA = 10000.0
LANES = 128
SUBLANES = 8
VMEM_LIMIT_BYTES = 62 * 1024 * 1024
PAGES_PER_CHUNK = 32
PAGES_PER_PIECE = 8
N_STREAMS = 2
N_SLOTS = 4
FF_CHUNK = 1024
POST_GROUP_ROWS = 512
POST_TILE_ROWS = 1024
PAGED_SEQS_PER_STEP = 2
FLASH_Q_TILES = 2
QK_LOOKAHEAD = 3
SUM_ROWS = 16
MASK_VALUE = -1e30
LOG2_E = math.log2(math.e)


def _rms(x, g):
    ms = jnp.mean(x * x, axis=-1, keepdims=True)
    return x * lax.rsqrt(ms + EPS) * g


def _const_spec(shape):
    nd = len(shape)
    return pl.BlockSpec(shape, lambda *_: (0,) * nd, pipeline_mode=pl.Buffered(1))


def _params(semantics):
    return pltpu.CompilerParams(dimension_semantics=semantics,
                                vmem_limit_bytes=VMEM_LIMIT_BYTES)


def _ab_project(x_ref, tab_ref, gmix_ref, win_ref, gq_ref, gkv_ref, cw_ref, rope):
    cw = cw_ref.shape[1]
    q_rank = gq_ref.shape[1]
    kv_rank = gkv_ref.shape[1]
    a = _rms(x_ref[...], gmix_ref[...]).astype(BF16)
    z = jnp.dot(a, win_ref[...], preferred_element_type=F32)
    o_q = 3 * cw
    o_kv = o_q + q_rank
    o_pe = o_kv + kv_rank
    up = z[:, cw:2 * cw] * z[:, 0:cw]
    gb = z[:, 2 * cw:o_q]
    ckv_n = _rms(z[:, o_kv:o_pe], gkv_ref[...])
    prod = z[:, o_pe:o_pe + LANES] * tab_ref[:, 0:LANES]
    kr = prod + pltpu.roll(prod, LANES - rope, axis=1)
    qn = _rms(z[:, o_q:o_kv], gq_ref[...]).astype(BF16)
    return up, gb, ckv_n, kr, qn


NT_DIMS = (((1,), (1,)), ((), ()))


def _ab_prompt_kernel(x_ref, tab_ref, gmix_ref, win_ref, gq_ref, gkv_ref, cw_ref,
                      tabt_ref, wqt_ref, wk_ref, wvt_ref,
                      yconv_ref, ckv_ref, kpe_ref, qt_ref, k_ref, vt_ref, cstate_ref,
                      carry_ref, *, tiles_per_seq, nope):
    i = pl.program_id(0)

    @pl.when(i % tiles_per_seq == 0)
    def _():
        carry_ref[...] = jnp.zeros_like(carry_ref)

    up, gb, ckv_n, kr, qn = _ab_project(
        x_ref, tab_ref, gmix_ref, win_ref, gq_ref, gkv_ref, cw_ref, kpe_ref.shape[1])
    tm = up.shape[0]
    c0 = carry_ref[SUBLANES - 2:SUBLANES - 1, :]
    c1 = carry_ref[SUBLANES - 1:SUBLANES, :]
    row = lax.broadcasted_iota(jnp.int32, (tm, 1), 0)
    um1 = jnp.where(row == 0, c1, pltpu.roll(up, 1, axis=0))
    um2 = jnp.where(row == 0, c0, jnp.where(row == 1, c1, pltpu.roll(up, 2, axis=0)))
    cw = cw_ref[...]
    conv = cw[0:1, :] * um2 + cw[1:2, :] * um1 + cw[2:3, :] * up
    yconv_ref[...] = (gb * conv).astype(yconv_ref.dtype)
    tail = up[tm - SUBLANES:, :]
    carry_ref[...] = tail
    cstate_ref[0] = tail

    ckv_ref[...] = ckv_n
    kpe_ref[...] = kr[:, 0:kpe_ref.shape[1]]
    rope = kpe_ref.shape[1]
    hr = rope // 2
    qt = lax.dot_general(wqt_ref[...], qn, NT_DIMS, preferred_element_type=F32)
    cos_t = tabt_ref[0:hr, :]
    sin_t = tabt_ref[hr:rope, :]
    pieces = []
    for h in range(qt.shape[0] // LANES):
        base = h * LANES
        x1 = qt[base + nope:base + nope + hr]
        x2 = qt[base + nope + hr:base + nope + rope]
        pieces += [qt[base:base + nope], x1 * cos_t - x2 * sin_t, x2 * cos_t + x1 * sin_t,
                   qt[base + nope + rope:base + LANES]]
    qt_ref[0] = jnp.concatenate(pieces, axis=0).astype(qt_ref.dtype)
    ckv_b = ckv_n.astype(BF16)
    lane = lax.broadcasted_iota(jnp.int32, kr.shape, 1)
    k_rot = jnp.where((lane >= nope) & (lane < nope + rope), pltpu.roll(kr, nope, axis=1), 0.0)
    k_cat = (jnp.dot(ckv_b, wk_ref[...], preferred_element_type=F32)
             + jnp.concatenate([k_rot] * (wk_ref.shape[1] // LANES), axis=1))
    k_ref[...] = k_cat.astype(k_ref.dtype)
    vt_ref[0] = lax.dot_general(wvt_ref[...], ckv_b, NT_DIMS,
                                preferred_element_type=F32).astype(vt_ref.dtype)


def _ab_sample_kernel(x_ref, tab_ref, gmix_ref, win_ref, gq_ref, gkv_ref, cw_ref,
                      wq_ref, wukt_ref, state_ref,
                      yconv_ref, ckv_ref, kpe_ref, q_ref, qlat_ref, up_ref,
                      prev1_ref, prev2_ref):
    t = pl.program_id(0)

    @pl.when(t == 0)
    def _():
        prev2_ref[...] = state_ref[0]
        prev1_ref[...] = state_ref[1]

    up, gb, ckv_n, kr, qn = _ab_project(
        x_ref, tab_ref, gmix_ref, win_ref, gq_ref, gkv_ref, cw_ref, kpe_ref.shape[1])
    q2 = jnp.dot(qn, wq_ref[...], preferred_element_type=F32)
    half = q2.shape[1] // 2
    cpat = jnp.concatenate([tab_ref[:, LANES:2 * LANES]] * (half // LANES), axis=1)
    spat = jnp.concatenate([tab_ref[:, 2 * LANES:3 * LANES]] * (half // LANES), axis=1)
    q_r = q2[:, :half] * cpat + q2[:, half:] * spat
    cw = cw_ref[...]
    prev1 = prev1_ref[...]
    conv = cw[0:1, :] * prev2_ref[...] + cw[1:2, :] * prev1 + cw[2:3, :] * up
    yconv_ref[...] = (gb * conv).astype(yconv_ref.dtype)
    up_ref[...] = up
    prev2_ref[...] = prev1
    prev1_ref[...] = up

    ckv_ref[...] = ckv_n
    kpe_ref[...] = kr[:, 0:kpe_ref.shape[1]]
    q_b = q_r.astype(BF16)
    q_ref[...] = q_b
    for h in range(qlat_ref.shape[0]):
        qlat_ref[h] = jnp.dot(q_b[:, h * LANES:(h + 1) * LANES], wukt_ref[h],
                              preferred_element_type=F32).astype(qlat_ref.dtype)


def _flash_kernel(qi_ref, kj_ref, qt_ref, k_ref, vt_ref, o_ref, m_ref, acc_ref, *,
                  c_exp, n_heads, v_dim):
    t = pl.program_id(1)
    qi = qi_ref[t]
    kj = kj_ref[t]
    tk = k_ref.shape[1]
    tq = tk
    n_q = qt_ref.shape[2] // tq

    @pl.when(kj == 0)
    def _():
        m_ref[...] = jnp.full_like(m_ref, MASK_VALUE)
        acc_ref[...] = jnp.zeros_like(acc_ref)

    def step(masked, qs):
        ones = jnp.ones((SUM_ROWS, tk), BF16)
        if masked:
            k_id = lax.broadcasted_iota(jnp.int32, (tk, tq), 0)
            q_id = lax.broadcasted_iota(jnp.int32, (tk, tq), 1)
            keep = k_id <= q_id
        def scores_t(h):
            k = k_ref[0, :, h * LANES:(h + 1) * LANES]
            qt = qt_ref[0, h * LANES:(h + 1) * LANES, qs]
            return jnp.dot(k, qt, preferred_element_type=F32)

        pending = [scores_t(h) for h in range(min(QK_LOOKAHEAD, n_heads))]
        for h in range(n_heads):
            st = pending.pop(0)
            if h + QK_LOOKAHEAD < n_heads:
                pending.append(scores_t(h + QK_LOOKAHEAD))
            if masked:
                st = jnp.where(keep, st, MASK_VALUE)
            m_old = m_ref[h, :, qs]
            m_new = jnp.maximum(m_old, jnp.max(st, axis=0, keepdims=True))
            alpha = jnp.exp2((m_old - m_new) * c_exp)
            pt = jnp.exp2((st - m_new) * c_exp).astype(BF16)
            vx = jnp.concatenate([vt_ref[0, h * v_dim:(h + 1) * v_dim, :], ones], axis=0)
            acc_ref[h, :, qs] = (alpha * acc_ref[h, :, qs]
                                 + jnp.dot(vx, pt, preferred_element_type=F32))
            m_ref[h, :, qs] = m_new

    for s in range(n_q):
        qs = slice(s * tq, (s + 1) * tq)
        q_tile = qi * n_q + s

        @pl.when(kj < q_tile)
        def _(qs=qs):
            step(False, qs)

        @pl.when(kj == q_tile)
        def _(qs=qs):
            step(True, qs)
            ot = jnp.concatenate(
                [acc_ref[h, 0:v_dim, qs] / acc_ref[h, v_dim:v_dim + 1, qs]
                 for h in range(n_heads)], axis=0)
            o_ref[0, qs, :] = ot.T.astype(o_ref.dtype)


def _paged_kernel(pt_ref, ql_ref, *refs, **static):
    per_step = ql_ref.shape[0]
    for e in range(per_step):
        _paged_sequence(pl.program_id(0) * per_step + e, pl.num_programs(0) * per_step, e,
                        pt_ref, ql_ref, *refs, **static)


def _paged_sequence(b, n_b, e, pt_ref, ql_ref, qp_ref, kn_ref, pn_ref, pool_c, pool_pt,
                    o_ref, cbuf, pbuf, sem_c, sem_p, *, c_exp, n_pages, n_heads, piece):
    cp = cbuf.shape[1]
    page = cbuf.shape[2]
    n_chunks = n_pages // cp

    def page_copies(pg, k, slot):
        return (pltpu.make_async_copy(pool_c.at[pg], cbuf.at[slot, k], sem_c.at[slot]),
                pltpu.make_async_copy(pool_pt.at[pg], pbuf.at[slot, :, pl.ds(k * page, page)],
                                      sem_p.at[slot]))

    def start_chunk(g, slot):
        for k in range(cp):
            for cpy in page_copies(pt_ref[g * cp + k], k, slot):
                cpy.start()

    def wait_chunk(slot):
        for k in range(cp):
            for cpy in page_copies(0, k, slot):
                cpy.wait()

    last_g = n_b * n_chunks - 1
    n_slots = cbuf.shape[0]
    ahead = n_slots - 1

    @pl.when(b == 0)
    def _():
        for a in range(ahead):
            start_chunk(jnp.minimum(a, last_g), a)

    ql = ql_ref[e]
    qp = qp_ref[e]

    kn = kn_ref[e].astype(BF16)
    pn = pn_ref[e].astype(BF16)
    s0 = (lax.dot_general(ql, kn, NT_DIMS, preferred_element_type=F32)
          + lax.dot_general(qp, pn, NT_DIMS, preferred_element_type=F32))
    r_id = lax.broadcasted_iota(jnp.int32, s0.shape, 0)
    c_id = lax.broadcasted_iota(jnp.int32, s0.shape, 1)
    s0 = jnp.where(c_id <= r_id // n_heads, s0, MASK_VALUE)
    m0 = jnp.max(s0, axis=1, keepdims=True)
    p0 = jnp.exp2((s0 - m0) * c_exp)
    l0 = jnp.sum(p0, axis=1, keepdims=True)
    acc0 = jnp.dot(p0.astype(BF16), kn, preferred_element_type=F32)

    def latent_piece(u, slot):
        kc = cbuf[slot, u * piece:(u + 1) * piece].reshape(piece * page, cbuf.shape[3])
        return kc.astype(BF16)

    def scores(u, slot):
        kpt = pbuf[slot, :, u * piece * page:(u + 1) * piece * page].astype(BF16)
        return (lax.dot_general(ql, latent_piece(u, slot), NT_DIMS, preferred_element_type=F32)
                + jnp.dot(qp, kpt, preferred_element_type=F32))

    def absorb(state, s, u, slot):
        m_run, l_run, acc = state
        m_new = jnp.maximum(m_run, jnp.max(s, axis=1, keepdims=True))
        alpha = jnp.exp2((m_run - m_new) * c_exp)
        p = jnp.exp2((s - m_new) * c_exp)
        l_new = alpha * l_run + jnp.sum(p, axis=1, keepdims=True)
        acc = alpha * acc + jnp.dot(p.astype(BF16), latent_piece(u, slot),
                                    preferred_element_type=F32)
        return m_new, l_new, acc

    def body(c, streams):
        g = b * n_chunks + c
        slot = g % n_slots
        wait_chunk(slot)
        start_chunk(jnp.minimum(g + ahead, last_g), (g + ahead) % n_slots)
        streams = list(streams)
        s_all = [scores(u, slot) for u in range(cp // piece)]
        for u, s in enumerate(s_all):
            streams[u % N_STREAMS] = absorb(streams[u % N_STREAMS], s, u, slot)
        return tuple(streams)

    empty = (jnp.full_like(m0, MASK_VALUE), jnp.zeros_like(l0), jnp.zeros_like(acc0))
    streams = lax.fori_loop(0, n_chunks, body, ((m0, l0, acc0),) + (empty,) * (N_STREAMS - 1))

    @pl.when(b == n_b - 1)
    def _():
        for a in range(1, n_slots):
            wait_chunk((last_g + a) % n_slots)

    m_fin = functools.reduce(jnp.maximum, [st[0] for st in streams])
    weights = [jnp.exp2((st[0] - m_fin) * c_exp) for st in streams]
    l_fin = sum(wt * st[1] for wt, st in zip(weights, streams))
    acc = sum(wt * st[2] for wt, st in zip(weights, streams))
    o_ref[e] = (acc / l_fin).astype(o_ref.dtype)


def _latent_out_kernel(o_ref, wuv_ref, y_ref, *, n_heads, v_dim):
    full = jnp.dot(o_ref[...], wuv_ref[...], preferred_element_type=F32)
    r_id = lax.broadcasted_iota(jnp.int32, full.shape, 0)
    c_id = lax.broadcasted_iota(jnp.int32, full.shape, 1)
    full = jnp.where(c_id // v_dim == r_id % n_heads, full, 0.0)
    y_ref[...] = jnp.sum(full.reshape(full.shape[0] // n_heads, n_heads, full.shape[1]),
                         axis=1).astype(y_ref.dtype)


def _gelu(x):
    return 0.5 * x * (1.0 + lax.erf(x * (1.0 / math.sqrt(2.0))))


def _gmlp_uv(x, g_mix, w_in, g_v):
    a = _rms(x, g_mix).astype(BF16)
    z = _gelu(jnp.dot(a, w_in, preferred_element_type=F32))
    width = z.shape[1] // 2
    return z[:, :width], _rms(z[:, width:], g_v)


def _gmlp_prompt_kernel(x_ref, gmix_ref, win_ref, gv_ref, ws_ref, bias_ref, m_ref, *, chunk,
                        n_sub):
    sub = x_ref.shape[0] // n_sub
    uvs = [_gmlp_uv(x_ref[r * sub:(r + 1) * sub, :], gmix_ref[...], win_ref[...], gv_ref[...])
           for r in range(n_sub)]
    n_groups = ws_ref.shape[0]
    r_id = lax.broadcasted_iota(jnp.int32, (chunk, chunk), 0)
    c_id = lax.broadcasted_iota(jnp.int32, (chunk, chunk), 1)
    tril = c_id <= r_id
    bias = bias_ref[...]
    gw = m_ref.shape[1] // n_groups
    n_chunks = sub // chunk
    w_tril = [jnp.where(tril, ws_ref[g], 0.0).astype(BF16) for g in range(n_groups)]
    for r, (u, v) in enumerate(uvs):
        vb = v.astype(BF16)
        for g in range(n_groups):
            gs = slice(g * gw, (g + 1) * gw)
            rhs = jnp.concatenate([vb[c * chunk:(c + 1) * chunk, gs] for c in range(n_chunks)],
                                  axis=1)
            s_all = jnp.dot(w_tril[g], rhs, preferred_element_type=F32)
            for c in range(n_chunks):
                rs = slice(c * chunk, (c + 1) * chunk)
                s = s_all[:, c * gw:(c + 1) * gw] + bias[:, gs]
                m_ref[r * sub + c * chunk:r * sub + (c + 1) * chunk, gs] = (
                    u[rs, gs] * s).astype(m_ref.dtype)


def _gmlp_sample_kernel(x_ref, gmix_ref, win_ref, gv_ref, ws_ref, bias_ref, m_ref, v_ref, *, n_t):
    u, v = _gmlp_uv(x_ref[...], gmix_ref[...], win_ref[...], gv_ref[...])
    v_ref[...] = v
    db = x_ref.shape[0] // n_t
    vb = v.astype(BF16).astype(F32)
    for t in range(n_t):
        s = bias_ref[t:t + 1, :]
        for k in range(t + 1):
            w = ws_ref[t, k:k + 1, :].astype(BF16).astype(F32)
            s = s + w * vb[k * db:(k + 1) * db, :]
        m_ref[t * db:(t + 1) * db, :] = (u[t * db:(t + 1) * db, :] * s).astype(m_ref.dtype)


def _post_kernel(*refs, n_mix, ff_chunk, final, n_sub):
    h_ref = refs[0]
    mix_refs = refs[1:1 + n_mix]
    (wout_ref, p_ref, gffn_ref, w1_ref, w2_ref, gple_ref, wg_ref, wple_ref, gf_ref,
     out_ref) = refs[1 + n_mix:]
    sub = h_ref.shape[0] // n_sub
    groups = [slice(r * sub, (r + 1) * sub) for r in range(n_sub)]

    hs = []
    for rs in groups:
        h = h_ref[rs, :]
        row = 0
        for m_ref in mix_refs:
            h = h + jnp.dot(m_ref[rs, :], wout_ref[row:row + m_ref.shape[1], :],
                            preferred_element_type=F32)
            row += m_ref.shape[1]
        hs.append(h)
    acts = [_rms(h, gffn_ref[...]).astype(BF16) for h in hs]
    ys = [None] * n_sub
    for c in range(w1_ref.shape[1] // ff_chunk):
        cs = slice(c * ff_chunk, (c + 1) * ff_chunk)
        for r in range(n_sub):
            hid = jnp.maximum(jnp.dot(acts[r], w1_ref[:, cs], preferred_element_type=F32), 0.0)
            part = jnp.dot((hid * hid).astype(BF16), w2_ref[cs, :], preferred_element_type=F32)
            ys[r] = part if ys[r] is None else ys[r] + part
    hs = [h + y for h, y in zip(hs, ys)]
    acts = [_rms(h, gple_ref[...]).astype(BF16) for h in hs]
    gates = [jax.nn.sigmoid(jnp.dot(a, wg_ref[...], preferred_element_type=F32)) for a in acts]
    for rs, h, gate in zip(groups, hs, gates):
        pe = jnp.dot(p_ref[rs, :].astype(BF16), wple_ref[...], preferred_element_type=F32)
        h = h + gate * pe
        if final:
            h = _rms(h, gf_ref[...])
        out_ref[rs, :] = h


def _row_spec(tm, width):
    return pl.BlockSpec((tm, width), lambda i, *_: (i, 0))


def _ab_common_specs(tm, d, tab_width, tab_map, w):
    return [
        _row_spec(tm, d),
        pl.BlockSpec((tm, tab_width), tab_map),
        _const_spec((1, d)),
        _const_spec(w['w_in'].shape),
        _const_spec(w['g_q'].shape),
        _const_spec(w['g_kv'].shape),
        _const_spec(w['conv_w'].shape),
    ]


def _ab_prompt(h, tab, tab_t, w, batch, seq, tm):
    n, d = h.shape
    tiles_per_seq = seq // tm
    conv_w = w['conv_w'].shape[1]
    kv_rank = w['g_kv'].shape[1]
    rope = w['rope_dim']
    qw = w['w_qt'].shape[0]
    vw = w['w_vt'].shape[0]
    in_specs = _ab_common_specs(tm, d, LANES, lambda i: (i % tiles_per_seq, 0), w) + [
        pl.BlockSpec((rope, tm), lambda i: (0, i % tiles_per_seq)),
        _const_spec(w['w_qt'].shape), _const_spec(w['w_k'].shape), _const_spec(w['w_vt'].shape)]
    out_shape = (
        jax.ShapeDtypeStruct((n, conv_w), BF16),
        jax.ShapeDtypeStruct((n, kv_rank), F32),
        jax.ShapeDtypeStruct((n, rope), F32),
        jax.ShapeDtypeStruct((batch, qw, seq), BF16),
        jax.ShapeDtypeStruct((n, qw), BF16),
        jax.ShapeDtypeStruct((batch, vw, seq), BF16),
        jax.ShapeDtypeStruct((batch, SUBLANES, conv_w), F32),
    )

    def feature_major(width):
        return pl.BlockSpec((1, width, tm), lambda i: (i // tiles_per_seq, 0, i % tiles_per_seq))

    out_specs = (
        _row_spec(tm, conv_w), _row_spec(tm, kv_rank), _row_spec(tm, rope),
        feature_major(qw), _row_spec(tm, qw), feature_major(vw),
        pl.BlockSpec((1, SUBLANES, conv_w), lambda i: (i // tiles_per_seq, 0, 0)),
    )
    return pl.pallas_call(
        functools.partial(_ab_prompt_kernel, tiles_per_seq=tiles_per_seq, nope=w['nope']),
        grid=(n // tm,), in_specs=in_specs, out_specs=out_specs, out_shape=out_shape,
        scratch_shapes=[pltpu.VMEM((SUBLANES, conv_w), F32)],
        compiler_params=_params(("arbitrary",)), name="ab_prompt",
    )(h, tab, w['g_mix'], w['w_in'], w['g_q'], w['g_kv'], w['conv_w'],
      tab_t, w['w_qt'], w['w_k'], w['w_vt'])


def _ab_sample(h, tab, w, state, n_t):
    n, d = h.shape
    db = n // n_t
    conv_w = w['conv_w'].shape[1]
    kv_rank = w['g_kv'].shape[1]
    rope = w['rope_dim']
    qw = w['w_q'].shape[1] // 2
    n_heads = w['w_ukt'].shape[0]
    in_specs = _ab_common_specs(db, d, 3 * LANES, lambda i: (i, 0), w) + [
        _const_spec(w['w_q'].shape), _const_spec(w['w_ukt'].shape), _const_spec(state.shape)]
    out_shape = (
        jax.ShapeDtypeStruct((n, conv_w), BF16),
        jax.ShapeDtypeStruct((n, kv_rank), F32),
        jax.ShapeDtypeStruct((n, rope), F32),
        jax.ShapeDtypeStruct((n, qw), BF16),
        jax.ShapeDtypeStruct((n_heads, n, kv_rank), BF16),
        jax.ShapeDtypeStruct((n, conv_w), F32),
    )
    out_specs = (
        _row_spec(db, conv_w), _row_spec(db, kv_rank), _row_spec(db, rope), _row_spec(db, qw),
        pl.BlockSpec((n_heads, db, kv_rank), lambda i: (0, i, 0)),
        _row_spec(db, conv_w),
    )
    return pl.pallas_call(
        _ab_sample_kernel,
        grid=(n_t,), in_specs=in_specs, out_specs=out_specs, out_shape=out_shape,
        scratch_shapes=[pltpu.VMEM((db, conv_w), F32), pltpu.VMEM((db, conv_w), F32)],
        compiler_params=_params(("arbitrary",)), name="ab_sample",
    )(h, tab, w['g_mix'], w['w_in'], w['g_q'], w['g_kv'], w['conv_w'],
      w['w_q'], w['w_ukt'], state)


def _flash(qt, k, vt, scale, n_heads, tq):
    batch, seq, qw = k.shape
    vw = vt.shape[1]
    v_dim = vw // n_heads
    n_q = FLASH_Q_TILES if (seq // tq) % FLASH_Q_TILES == 0 else 1
    bq = n_q * tq
    pairs = [(i, j) for i in range(seq // bq) for j in range((i + 1) * n_q)]
    qi = jnp.asarray([p[0] for p in pairs], jnp.int32)
    kj = jnp.asarray([p[1] for p in pairs], jnp.int32)
    grid_spec = pltpu.PrefetchScalarGridSpec(
        num_scalar_prefetch=2,
        grid=(batch, len(pairs)),
        in_specs=[
            pl.BlockSpec((1, qw, bq), lambda b, t, qi, kj: (b, 0, qi[t])),
            pl.BlockSpec((1, tq, qw), lambda b, t, qi, kj: (b, kj[t], 0)),
            pl.BlockSpec((1, vw, tq), lambda b, t, qi, kj: (b, 0, kj[t])),
        ],
        out_specs=pl.BlockSpec((1, bq, vw), lambda b, t, qi, kj: (b, qi[t], 0)),
        scratch_shapes=[pltpu.VMEM((n_heads, 1, bq), F32),
                        pltpu.VMEM((n_heads, v_dim + SUM_ROWS, bq), F32)],
    )
    return pl.pallas_call(
        functools.partial(_flash_kernel, c_exp=scale * LOG2_E, n_heads=n_heads, v_dim=v_dim),
        grid_spec=grid_spec,
        out_shape=jax.ShapeDtypeStruct((batch, seq, vw), BF16),
        compiler_params=_params(("arbitrary", "arbitrary")), name="flash_prompt",
    )(qi, kj, qt, k, vt)


def _paged(page_table, ql, qp, kn, pn, w_uv, pool_c, pool_pt, scale, n_heads):
    db, rows, kv_rank = ql.shape
    rope = qp.shape[2]
    n_pages = page_table.shape[1]
    page = pool_c.shape[1]
    vw = w_uv.shape[1]
    cp = min(PAGES_PER_CHUNK, n_pages)
    piece = min(PAGES_PER_PIECE, cp)
    assert n_pages % cp == 0 and cp % piece == 0
    per_step = PAGED_SEQS_PER_STEP if db % PAGED_SEQS_PER_STEP == 0 else 1

    def seq_block(r, c):
        return pl.BlockSpec((per_step, r, c), lambda b, pt: (b, 0, 0))

    grid_spec = pltpu.PrefetchScalarGridSpec(
        num_scalar_prefetch=1, grid=(db // per_step,),
        in_specs=[seq_block(rows, kv_rank), seq_block(rows, rope), seq_block(SUBLANES, kv_rank),
                  seq_block(SUBLANES, rope),
                  pl.BlockSpec(memory_space=pl.ANY), pl.BlockSpec(memory_space=pl.ANY)],
        out_specs=seq_block(rows, kv_rank),
        scratch_shapes=[pltpu.VMEM((N_SLOTS, cp, page, kv_rank), F32),
                        pltpu.VMEM((N_SLOTS, rope, cp * page), F32),
                        pltpu.SemaphoreType.DMA((N_SLOTS,)),
                        pltpu.SemaphoreType.DMA((N_SLOTS,))])
    o_lat = pl.pallas_call(
        functools.partial(_paged_kernel, c_exp=scale * LOG2_E, n_pages=n_pages, n_heads=n_heads,
                          piece=piece),
        grid_spec=grid_spec, out_shape=jax.ShapeDtypeStruct((db, rows, kv_rank), BF16),
        compiler_params=_params(("arbitrary",)), name="paged_attn",
    )(page_table.reshape(-1), ql, qp, kn, pn, pool_c, pool_pt)
    n_tok = db * rows // n_heads
    return pl.pallas_call(
        functools.partial(_latent_out_kernel, n_heads=n_heads, v_dim=vw // n_heads),
        grid=(1,),
        in_specs=[_const_spec((db * rows, kv_rank)), _const_spec(w_uv.shape)],
        out_specs=_const_spec((n_tok, vw)),
        out_shape=jax.ShapeDtypeStruct((n_tok, vw), BF16),
        compiler_params=_params(("arbitrary",)), name="latent_out",
    )(o_lat.reshape(db * rows, kv_rank), w_uv)


def _gmlp_prompt(h, w, tm, chunk):
    n, d = h.shape
    width = w['w_in_c'].shape[1] // 2
    return pl.pallas_call(
        functools.partial(_gmlp_prompt_kernel, chunk=chunk, n_sub=max(1, tm // POST_GROUP_ROWS)),
        grid=(n // tm,),
        in_specs=[_row_spec(tm, d), _const_spec((1, d)), _const_spec(w['w_in_c'].shape),
                  _const_spec((1, width)), _const_spec(w['w_s'].shape),
                  _const_spec(w['bias_full'].shape)],
        out_specs=_row_spec(tm, width),
        out_shape=jax.ShapeDtypeStruct((n, width), BF16),
        compiler_params=_params(("arbitrary",)), name="gmlp_prompt",
    )(h, w['g_mix'], w['w_in_c'], w['g_v'], w['w_s'], w['bias_full'])


def _gmlp_sample(h, w, n_t):
    n, d = h.shape
    width = w['w_in_c'].shape[1] // 2
    return pl.pallas_call(
        functools.partial(_gmlp_sample_kernel, n_t=n_t),
        grid=(1,),
        in_specs=[_const_spec((n, d)), _const_spec((1, d)), _const_spec(w['w_in_c'].shape),
                  _const_spec((1, width)), _const_spec(w['ws_small'].shape),
                  _const_spec(w['bias_small'].shape)],
        out_specs=(_const_spec((n, width)), _const_spec((n, width))),
        out_shape=(jax.ShapeDtypeStruct((n, width), BF16), jax.ShapeDtypeStruct((n, width), F32)),
        compiler_params=_params(("arbitrary",)), name="gmlp_sample",
    )(h, w['g_mix'], w['w_in_c'], w['g_v'], w['ws_small'], w['bias_small'])


def _post(h, mixes, w_out, p, layer, w, tm, final, name):
    n, d = h.shape
    in_specs = ([_row_spec(tm, d)] + [_row_spec(tm, m.shape[1]) for m in mixes]
                + [_const_spec(w_out.shape),
                   pl.BlockSpec((None, tm, p.shape[2]), lambda i: (layer, i, 0)),
                   _const_spec((1, d)), _const_spec(w['w1'].shape),
                   _const_spec(w['w2'].shape), _const_spec((1, d)), _const_spec(w['w_gate'].shape),
                   _const_spec(w['w_ple'].shape), _const_spec((1, d))])
    return pl.pallas_call(
        functools.partial(_post_kernel, n_mix=len(mixes), ff_chunk=min(FF_CHUNK, w['w1'].shape[1]),
                          final=final, n_sub=max(1, tm // POST_GROUP_ROWS)),
        grid=(n // tm,), in_specs=in_specs, out_specs=_row_spec(tm, d),
        out_shape=jax.ShapeDtypeStruct((n, d), F32),
        compiler_params=_params(("arbitrary",)), name=name,
    )(h, *mixes, w_out, p, w['g_ffn'], w['w1'], w['w2'], w['g_ple'], w['w_gate'], w['w_ple'],
      w['g_f'])


def _rope_table(pos, rope_dim):
    inv = ROPE_THETA ** (-jnp.arange(0, rope_dim, 2, dtype=F32) / rope_dim)
    ang = pos.astype(F32)[:, None] * inv[None, :]
    c, s = jnp.cos(ang), jnp.sin(ang)
    t = pos.shape[0]
    cc = jnp.concatenate([c, c], axis=1)
    ss = jnp.concatenate([-s, s], axis=1)
    zeros = lambda w_: jnp.zeros((t, w_), F32)
    nope = LANES - 2 * rope_dim
    key_tab = jnp.concatenate([cc, ss, zeros(LANES - 2 * rope_dim)], axis=1)
    q_cos = jnp.concatenate([jnp.ones((t, nope), F32), cc, zeros(LANES - nope - rope_dim)], axis=1)
    q_sin = jnp.concatenate([zeros(nope), ss, zeros(LANES - nope - rope_dim)], axis=1)
    return jnp.concatenate([key_tab, q_cos, q_sin], axis=1)


def _prep_layer_ab(j, norm_mix_i, w_in_ab, conv_w, q_norm, w_uq, kv_norm, w_uk, w_uv):
    d = w_in_ab.shape[1]
    kv_rank, n_heads, nope = w_uk.shape[1:]
    rope = w_uq.shape[2] // n_heads - nope
    v_dim = w_uv.shape[3]
    half = rope // 2
    in_ab = w_in_ab.shape[2]
    w_in = w_in_ab[j]
    kpe0 = in_ab - rope
    pad = (-(in_ab + rope)) % (2 * LANES)
    w_in_p = jnp.concatenate(
        [w_in, w_in[:, kpe0 + half:], w_in[:, kpe0:kpe0 + half], jnp.zeros((d, pad), F32)], axis=1)
    q_rank = w_uq.shape[1]
    uq = w_uq[j].reshape(q_rank, n_heads, nope + rope)
    zpad = jnp.zeros((q_rank, n_heads, LANES - nope - rope), F32)
    wq_a = jnp.concatenate([uq, zpad], axis=2)
    wq_b = jnp.concatenate([jnp.zeros((q_rank, n_heads, nope), F32), uq[:, :, nope + half:],
                            uq[:, :, nope:nope + half], zpad], axis=2)
    w_q = jnp.concatenate([wq_a.reshape(q_rank, -1), wq_b.reshape(q_rank, -1)], axis=1)
    uk = w_uk[j]
    w_k = jnp.concatenate([uk, jnp.zeros((kv_rank, n_heads, LANES - nope), F32)], axis=2)
    w_ukt = jnp.concatenate([uk.transpose(1, 2, 0),
                             jnp.zeros((n_heads, LANES - nope, kv_rank), F32)], axis=1)
    return dict(
        g_mix=norm_mix_i.reshape(1, -1), w_in=w_in_p.astype(BF16), g_q=q_norm[j].reshape(1, -1),
        w_q=w_q.astype(BF16), w_qt=wq_a.reshape(q_rank, -1).T.astype(BF16),
        g_kv=kv_norm[j].reshape(1, -1), conv_w=conv_w[j],
        w_k=w_k.reshape(kv_rank, -1).astype(BF16),
        w_v=w_uv[j].reshape(kv_rank, n_heads * v_dim).astype(BF16),
        w_vt=w_uv[j].reshape(kv_rank, n_heads * v_dim).T.astype(BF16), w_ukt=w_ukt.astype(BF16),
        rope_dim=rope, n_heads=n_heads, nope=nope)


def _prep_layer_c(j, norm_mix_i, w_in_c, v_norm, w_s, b_s, n_t):
    n_groups, chunk = w_s.shape[1], w_s.shape[2]
    width = w_in_c.shape[2] // 2
    gw = width // n_groups
    bias_full = jnp.repeat(b_s[j].T, gw, axis=1)
    ws_small = jnp.repeat(w_s[j][:, :n_t, :n_t].transpose(1, 2, 0), gw, axis=2)
    return dict(g_mix=norm_mix_i.reshape(1, -1), w_in_c=w_in_c[j].astype(BF16),
                g_v=v_norm[j].reshape(1, -1), w_s=w_s[j], bias_full=bias_full,
                ws_small=ws_small, bias_small=bias_full[:n_t], chunk=chunk)


def _prep_post(i, norm_ffn, w_ff1, w_ff2, norm_ple, w_ple_gate, w_ple, norm_f):
    return dict(g_ffn=norm_ffn[i].reshape(1, -1), w1=w_ff1[i].astype(BF16), w2=w_ff2[i].astype(BF16),
                g_ple=norm_ple[i].reshape(1, -1), w_gate=w_ple_gate[i].astype(BF16),
                w_ple=w_ple[i].astype(BF16), g_f=norm_f.reshape(1, -1))


def _token_tile(n, cap):
    tm = min(cap, n)
    while n % tm:
        tm //= 2
    return tm


def kernel(x_prompt, x_sample, cache_ckv, cache_kpe, state_conv, page_table, p_prompt, p_sample, norm_mix, w_in_ab, conv_w, q_norm, w_uq, kv_norm, w_uk, w_uv, w_out_ab, w_in_c, v_norm, w_s, b_s, w_out_c, norm_ffn, w_ff1, w_ff2, norm_ple, w_ple_gate, w_ple, norm_f):
    depth = norm_mix.shape[0]
    batch, seq, d = x_prompt.shape
    db, n_t, _ = x_sample.shape
    n_pages = page_table.shape[1]
    page = cache_ckv.shape[2]
    past_len = n_pages * page
    n_heads, nope = w_uk.shape[2], w_uk.shape[3]
    rope = cache_kpe.shape[3]
    scale = float(nope + rope) ** -0.5

    tab_p = _rope_table(jnp.arange(seq, dtype=jnp.int32), rope)
    tab_s = jnp.repeat(_rope_table(past_len + jnp.arange(n_t, dtype=jnp.int32), rope), db, axis=0)

    n_p = batch * seq
    n_s = db * n_t
    tm_p = _token_tile(seq, 512)
    tm_post = _token_tile(seq, POST_TILE_ROWS)
    tq = _token_tile(seq, 512)
    hp = x_prompt.reshape(n_p, d)
    hs = x_sample.transpose(1, 0, 2).reshape(n_s, d)
    pp = p_prompt.reshape(depth, n_p, -1)
    ps = p_sample.transpose(0, 2, 1, 3).reshape(depth, n_s, -1)

    def unmajor(a):
        return a.reshape(n_t, db, a.shape[-1]).transpose(1, 0, 2)

    conv_p, conv_s, ckv_p, kpe_p, ckv_s, kpe_s, v_s = [], [], [], [], [], [], []
    for i in range(depth):
        j = i // 2
        wpost = _prep_post(i, norm_ffn, w_ff1, w_ff2, norm_ple, w_ple_gate, w_ple, norm_f)
        final = i == depth - 1
        if i % 2 == 0:
            w = _prep_layer_ab(j, norm_mix[i], w_in_ab, conv_w, q_norm, w_uq, kv_norm, w_uk, w_uv)
            w_out = w_out_ab[j].astype(BF16)
            hr = rope // 2
            cos_sin_t = jnp.concatenate([tab_p[:, :hr], tab_p[:, rope + hr:2 * rope]], axis=1).T
            yconv, ckv, kpe, qt, k, vt, cstate = _ab_prompt(
                hp, tab_p[:, :LANES], cos_sin_t, w, batch, seq, tm_p)
            yatt_p = _flash(qt, k.reshape(batch, seq, -1), vt, scale, n_heads, tq)
            conv_p.append(cstate[:, SUBLANES - 2:, :])
            ckv_p.append(ckv.reshape(batch, seq, -1))
            kpe_p.append(kpe.reshape(batch, seq, -1))
            state = state_conv[j].transpose(1, 0, 2)
            yconv_s, ckv, kpe, q, qlat, up = _ab_sample(hs, tab_s, w, state, n_t)
            kv_rank = ckv.shape[1]
            ql = qlat.reshape(n_heads, n_t, db, kv_rank).transpose(2, 1, 0, 3).reshape(
                db, n_t * n_heads, kv_rank)
            qp = q.reshape(n_t, db, n_heads, LANES)[..., nope:nope + rope].transpose(
                1, 0, 2, 3).reshape(db, n_t * n_heads, rope)
            ckv_b = unmajor(ckv)
            kpe_b = unmajor(kpe)
            padr = ((0, 0), (0, SUBLANES - n_t), (0, 0))
            hp = _post(hp, [yconv, yatt_p.reshape(n_p, -1)], w_out, pp, i, wpost, tm_post, final,
                       "post_prompt")
            yatt = _paged(page_table, ql, qp, jnp.pad(ckv_b, padr), jnp.pad(kpe_b, padr),
                          w['w_v'], cache_ckv[j], jnp.swapaxes(cache_kpe[j], 1, 2), scale, n_heads)
            yatt = yatt.reshape(db, n_t, -1).transpose(1, 0, 2).reshape(n_s, -1)
            hs = _post(hs, [yconv_s, yatt], w_out, ps, i, wpost, n_s, final, "post_sample")
            conv_s.append(unmajor(up)[:, n_t - 2:, :])
            ckv_s.append(ckv_b)
            kpe_s.append(kpe_b)
        else:
            w = _prep_layer_c(j, norm_mix[i], w_in_c, v_norm, w_s, b_s, n_t)
            w_out = w_out_c[j].astype(BF16)
            m = _gmlp_prompt(hp, w, tm_post, w['chunk'])
            hp = _post(hp, [m], w_out, pp, i, wpost, tm_post, final, "post_prompt")
            m, v = _gmlp_sample(hs, w, n_t)
            hs = _post(hs, [m], w_out, ps, i, wpost, n_s, final, "post_sample")
            v_s.append(unmajor(v))

    y_prompt = hp.reshape(batch, seq, d)
    y_sample = unmajor(hs)
    return (y_prompt, y_sample, jnp.stack(conv_p), jnp.stack(conv_s), jnp.stack(ckv_p),
            jnp.stack(kpe_p), jnp.stack(ckv_s), jnp.stack(kpe_s), jnp.stack(v_s))
```

```python
import functools
import math

import jax
import jax.numpy as jnp
from jax import lax
from jax.experimental import pallas as pl
from jax.experimental.pallas import tpu as pltpu

F32 = jnp.float32
BF16 = jnp.bfloat16

EPS = 1e-6
ROPE_THETA = 10000.0
LANES = 128
SUBLANES = 8
VMEM_LIMIT_BYTES = 62 * 1024 * 1024
PAGES_PER_CHUNK = 32
PAGES_PER_PIECE = 8
N_STREAMS = 2
N_SLOTS = 4
FF_CHUNK = 1024
POST_GROUP_ROWS = 512
POST_TILE_ROWS = 1024
PAGED_SEQS_PER_STEP = 4
FLASH_Q_TILES = 4
QK_LOOKAHEAD = 3
SUM_ROWS = 16
MASK_VALUE = -1e30
LOG2_E = math.log2(math.e)


def _rms(x, g):
    ms = jnp.mean(x * x, axis=-1, keepdims=True)
    return x * lax.rsqrt(ms + EPS) * g


def _const_spec(shape):
    nd = len(shape)
    return pl.BlockSpec(shape, lambda *_: (0,) * nd, pipeline_mode=pl.Buffered(1))


def _params(semantics):
    return pltpu.CompilerParams(dimension_semantics=semantics,
                                vmem_limit_bytes=VMEM_LIMIT_BYTES)


def _ab_project(x_ref, tab_ref, gmix_ref, win_ref, gq_ref, gkv_ref, cw_ref, rope):
    cw = cw_ref.shape[1]
    q_rank = gq_ref.shape[1]
    kv_rank = gkv_ref.shape[1]
    a = _rms(x_ref[...], gmix_ref[...]).astype(BF16)
    z = jnp.dot(a, win_ref[...], preferred_element_type=F32)
    o_q = 3 * cw
    o_kv = o_q + q_rank
    o_pe = o_kv + kv_rank
    up = z[:, cw:2 * cw] * z[:, 0:cw]
    gb = z[:, 2 * cw:o_q]
    ckv_n = _rms(z[:, o_kv:o_pe], gkv_ref[...])
    prod = z[:, o_pe:o_pe + LANES] * tab_ref[:, 0:LANES]
    kr = prod + pltpu.roll(prod, LANES - rope, axis=1)
    qn = _rms(z[:, o_q:o_kv], gq_ref[...]).astype(BF16)
    return up, gb, ckv_n, kr, qn


NT_DIMS = (((1,), (1,)), ((), ()))


def _ab_prompt_kernel(x_ref, tab_ref, gmix_ref, win_ref, gq_ref, gkv_ref, cw_ref,
                      tabt_ref, wqt_ref, wk_ref, wvt_ref,
                      yconv_ref, ckv_ref, kpe_ref, qt_ref, k_ref, vt_ref, cstate_ref,
                      carry_ref, *, tiles_per_seq, nope):
    i = pl.program_id(0)

    @pl.when(i % tiles_per_seq == 0)
    def _():
        carry_ref[...] = jnp.zeros_like(carry_ref)

    up, gb, ckv_n, kr, qn = _ab_project(
        x_ref, tab_ref, gmix_ref, win_ref, gq_ref, gkv_ref, cw_ref, kpe_ref.shape[1])
    tm = up.shape[0]
    c0 = carry_ref[SUBLANES - 2:SUBLANES - 1, :]
    c1 = carry_ref[SUBLANES - 1:SUBLANES, :]
    row = lax.broadcasted_iota(jnp.int32, (tm, 1), 0)
    um1 = jnp.where(row == 0, c1, pltpu.roll(up, 1, axis=0))
    um2 = jnp.where(row == 0, c0, jnp.where(row == 1, c1, pltpu.roll(up, 2, axis=0)))
    cw = cw_ref[...]
    conv = cw[0:1, :] * um2 + cw[1:2, :] * um1 + cw[2:3, :] * up
    yconv_ref[...] = (gb * conv).astype(yconv_ref.dtype)
    tail = up[tm - SUBLANES:, :]
    carry_ref[...] = tail
    cstate_ref[0] = tail

    ckv_ref[...] = ckv_n
    kpe_ref[...] = kr[:, 0:kpe_ref.shape[1]]
    rope = kpe_ref.shape[1]
    hr = rope // 2
    qt = lax.dot_general(wqt_ref[...], qn, NT_DIMS, preferred_element_type=F32)
    cos_t = tabt_ref[0:hr, :]
    sin_t = tabt_ref[hr:rope, :]
    pieces = []
    for h in range(qt.shape[0] // LANES):
        base = h * LANES
        x1 = qt[base + nope:base + nope + hr]
        x2 = qt[base + nope + hr:base + nope + rope]
        pieces += [qt[base:base + nope], x1 * cos_t - x2 * sin_t, x2 * cos_t + x1 * sin_t,
                   qt[base + nope + rope:base + LANES]]
    qt_ref[0] = jnp.concatenate(pieces, axis=0).astype(qt_ref.dtype)
    ckv_b = ckv_n.astype(BF16)
    lane = lax.broadcasted_iota(jnp.int32, kr.shape, 1)
    k_rot = jnp.where((lane >= nope) & (lane < nope + rope), pltpu.roll(kr, nope, axis=1), 0.0)
    k_cat = (jnp.dot(ckv_b, wk_ref[...], preferred_element_type=F32)
             + jnp.concatenate([k_rot] * (wk_ref.shape[1] // LANES), axis=1))
    k_ref[...] = k_cat.astype(k_ref.dtype)
    vt_ref[0] = lax.dot_general(wvt_ref[...], ckv_b, NT_DIMS,
                                preferred_element_type=F32).astype(vt_ref.dtype)


def _ab_sample_kernel(x_ref, tab_ref, gmix_ref, win_ref, gq_ref, gkv_ref, cw_ref,
                      wq_ref, wukt_ref, state_ref,
                      yconv_ref, ckv_ref, kpe_ref, q_ref, qlat_ref, up_ref,
                      prev1_ref, prev2_ref):
    t = pl.program_id(0)

    @pl.when(t == 0)
    def _():
        prev2_ref[...] = state_ref[0]
        prev1_ref[...] = state_ref[1]

    up, gb, ckv_n, kr, qn = _ab_project(
        x_ref, tab_ref, gmix_ref, win_ref, gq_ref, gkv_ref, cw_ref, kpe_ref.shape[1])
    q2 = jnp.dot(qn, wq_ref[...], preferred_element_type=F32)
    half = q2.shape[1] // 2
    cpat = jnp.concatenate([tab_ref[:, LANES:2 * LANES]] * (half // LANES), axis=1)
    spat = jnp.concatenate([tab_ref[:, 2 * LANES:3 * LANES]] * (half // LANES), axis=1)
    q_r = q2[:, :half] * cpat + q2[:, half:] * spat
    cw = cw_ref[...]
    prev1 = prev1_ref[...]
    conv = cw[0:1, :] * prev2_ref[...] + cw[1:2, :] * prev1 + cw[2:3, :] * up
    yconv_ref[...] = (gb * conv).astype(yconv_ref.dtype)
    up_ref[...] = up
    prev2_ref[...] = prev1
    prev1_ref[...] = up

    ckv_ref[...] = ckv_n
    kpe_ref[...] = kr[:, 0:kpe_ref.shape[1]]
    q_b = q_r.astype(BF16)
    q_ref[...] = q_b
    for h in range(qlat_ref.shape[0]):
        qlat_ref[h] = jnp.dot(q_b[:, h * LANES:(h + 1) * LANES], wukt_ref[h],
                              preferred_element_type=F32).astype(qlat_ref.dtype)


def _flash_kernel(qi_ref, kj_ref, qt_ref, k_ref, vt_ref, o_ref, m_ref, acc_ref, *,
                  c_exp, n_heads, v_dim):
    t = pl.program_id(1)
    qi = qi_ref[t]
    kj = kj_ref[t]
    tk = k_ref.shape[1]
    tq = tk
    n_q = qt_ref.shape[2] // tq

    @pl.when(kj == 0)
    def _():
        m_ref[...] = jnp.full_like(m_ref, MASK_VALUE)
        acc_ref[...] = jnp.zeros_like(acc_ref)

    def step(masked, qs):
        ones = jnp.ones((SUM_ROWS, tk), BF16)
        if masked:
            k_id = lax.broadcasted_iota(jnp.int32, (tk, tq), 0)
            q_id = lax.broadcasted_iota(jnp.int32, (tk, tq), 1)
            keep = k_id <= q_id
        def scores_t(h):
            k = k_ref[0, :, h * LANES:(h + 1) * LANES]
            qt = qt_ref[0, h * LANES:(h + 1) * LANES, qs]
            return jnp.dot(k, qt, preferred_element_type=F32)

        pending = [scores_t(h) for h in range(min(QK_LOOKAHEAD, n_heads))]
        for h in range(n_heads):
            st = pending.pop(0)
            if h + QK_LOOKAHEAD < n_heads:
                pending.append(scores_t(h + QK_LOOKAHEAD))
            if masked:
                st = jnp.where(keep, st, MASK_VALUE)
            m_old = m_ref[h, :, qs]
            m_new = jnp.maximum(m_old, jnp.max(st, axis=0, keepdims=True))
            alpha = jnp.exp2((m_old - m_new) * c_exp)
            pt = jnp.exp2((st - m_new) * c_exp).astype(BF16)
            vx = jnp.concatenate([vt_ref[0, h * v_dim:(h + 1) * v_dim, :], ones], axis=0)
            acc_ref[h, :, qs] = (alpha * acc_ref[h, :, qs]
                                 + jnp.dot(vx, pt, preferred_element_type=F32))
            m_ref[h, :, qs] = m_new

    for s in range(n_q):
        qs = slice(s * tq, (s + 1) * tq)
        q_tile = qi * n_q + s

        @pl.when(kj < q_tile)
        def _(qs=qs):
            step(False, qs)

        @pl.when(kj == q_tile)
        def _(qs=qs):
            step(True, qs)
            ot = jnp.concatenate(
                [acc_ref[h, 0:v_dim, qs] / acc_ref[h, v_dim:v_dim + 1, qs]
                 for h in range(n_heads)], axis=0)
            o_ref[0, qs, :] = ot.T.astype(o_ref.dtype)


def _paged_kernel(pt_ref, ql_ref, *refs, **static):
    per_step = ql_ref.shape[0]
    for e in range(per_step):
        _paged_sequence(pl.program_id(0) * per_step + e, pl.num_programs(0) * per_step, e,
                        pt_ref, ql_ref, *refs, **static)


def _paged_sequence(b, n_b, e, pt_ref, ql_ref, qp_ref, kn_ref, pn_ref, pool_c, pool_pt,
                    o_ref, cbuf, pbuf, sem_c, sem_p, *, c_exp, n_pages, n_heads, piece):
    cp = cbuf.shape[1]
    page = cbuf.shape[2]
    n_chunks = n_pages // cp

    def page_copies(pg, k, slot):
        return (pltpu.make_async_copy(pool_c.at[pg], cbuf.at[slot, k], sem_c.at[slot]),
                pltpu.make_async_copy(pool_pt.at[pg], pbuf.at[slot, :, pl.ds(k * page, page)],
                                      sem_p.at[slot]))

    def start_chunk(g, slot):
        for k in range(cp):
            for cpy in page_copies(pt_ref[g * cp + k], k, slot):
                cpy.start()

    def wait_chunk(slot):
        for k in range(cp):
            for cpy in page_copies(0, k, slot):
                cpy.wait()

    last_g = n_b * n_chunks - 1
    n_slots = cbuf.shape[0]
    ahead = n_slots - 1

    @pl.when(b == 0)
    def _():
        for a in range(ahead):
            start_chunk(jnp.minimum(a, last_g), a)

    ql = ql_ref[e]
    qp = qp_ref[e]

    kn = kn_ref[e].astype(BF16)
    pn = pn_ref[e].astype(BF16)
    s0 = (lax.dot_general(ql, kn, NT_DIMS, preferred_element_type=F32)
          + lax.dot_general(qp, pn, NT_DIMS, preferred_element_type=F32))
    r_id = lax.broadcasted_iota(jnp.int32, s0.shape, 0)
    c_id = lax.broadcasted_iota(jnp.int32, s0.shape, 1)
    s0 = jnp.where(c_id <= r_id // n_heads, s0, MASK_VALUE)
    m0 = jnp.max(s0, axis=1, keepdims=True)
    p0 = jnp.exp2((s0 - m0) * c_exp)
    l0 = jnp.sum(p0, axis=1, keepdims=True)
    acc0 = jnp.dot(p0.astype(BF16), kn, preferred_element_type=F32)

    def latent_piece(u, slot):
        kc = cbuf[slot, u * piece:(u + 1) * piece].reshape(piece * page, cbuf.shape[3])
        return kc.astype(BF16)

    def scores(u, slot):
        kpt = pbuf[slot, :, u * piece * page:(u + 1) * piece * page].astype(BF16)
        return (lax.dot_general(ql, latent_piece(u, slot), NT_DIMS, preferred_element_type=F32)
                + jnp.dot(qp, kpt, preferred_element_type=F32))

    def absorb(state, s, u, slot):
        m_run, l_run, acc = state
        m_new = jnp.maximum(m_run, jnp.max(s, axis=1, keepdims=True))
        alpha = jnp.exp2((m_run - m_new) * c_exp)
        p = jnp.exp2((s - m_new) * c_exp)
        l_new = alpha * l_run + jnp.sum(p, axis=1, keepdims=True)
        acc = alpha * acc + jnp.dot(p.astype(BF16), latent_piece(u, slot),
                                    preferred_element_type=F32)
        return m_new, l_new, acc

    def body(c, streams):
        g = b * n_chunks + c
        slot = g % n_slots
        wait_chunk(slot)
        start_chunk(jnp.minimum(g + ahead, last_g), (g + ahead) % n_slots)
        streams = list(streams)
        s_all = [scores(u, slot) for u in range(cp // piece)]
        for u, s in enumerate(s_all):
            streams[u % N_STREAMS] = absorb(streams[u % N_STREAMS], s, u, slot)
        return tuple(streams)

    empty = (jnp.full_like(m0, MASK_VALUE), jnp.zeros_like(l0), jnp.zeros_like(acc0))
    streams = lax.fori_loop(0, n_chunks, body, ((m0, l0, acc0),) + (empty,) * (N_STREAMS - 1))

    @pl.when(b == n_b - 1)
    def _():
        for a in range(1, n_slots):
            wait_chunk((last_g + a) % n_slots)

    m_fin = functools.reduce(jnp.maximum, [st[0] for st in streams])
    weights = [jnp.exp2((st[0] - m_fin) * c_exp) for st in streams]
    l_fin = sum(wt * st[1] for wt, st in zip(weights, streams))
    acc = sum(wt * st[2] for wt, st in zip(weights, streams))
    o_ref[e] = (acc / l_fin).astype(o_ref.dtype)


def _latent_out_kernel(o_ref, wuv_ref, y_ref, *, n_heads, v_dim):
    full = jnp.dot(o_ref[...], wuv_ref[...], preferred_element_type=F32)
    r_id = lax.broadcasted_iota(jnp.int32, full.shape, 0)
    c_id = lax.broadcasted_iota(jnp.int32, full.shape, 1)
    full = jnp.where(c_id // v_dim == r_id % n_heads, full, 0.0)
    y_ref[...] = jnp.sum(full.reshape(full.shape[0] // n_heads, n_heads, full.shape[1]),
                         axis=1).astype(y_ref.dtype)


def _gelu(x):
    return 0.5 * x * (1.0 + lax.erf(x * (1.0 / math.sqrt(2.0))))


def _gmlp_uv(x, g_mix, w_in, g_v):
    a = _rms(x, g_mix).astype(BF16)
    z = _gelu(jnp.dot(a, w_in, preferred_element_type=F32))
    width = z.shape[1] // 2
    return z[:, :width], _rms(z[:, width:], g_v)


def _gmlp_prompt_kernel(x_ref, gmix_ref, win_ref, gv_ref, ws_ref, bias_ref, m_ref, *, chunk,
                        n_sub):
    sub = x_ref.shape[0] // n_sub
    uvs = [_gmlp_uv(x_ref[r * sub:(r + 1) * sub, :], gmix_ref[...], win_ref[...], gv_ref[...])
           for r in range(n_sub)]
    n_groups = ws_ref.shape[0]
    r_id = lax.broadcasted_iota(jnp.int32, (chunk, chunk), 0)
    c_id = lax.broadcasted_iota(jnp.int32, (chunk, chunk), 1)
    tril = c_id <= r_id
    bias = bias_ref[...]
    gw = m_ref.shape[1] // n_groups
    n_chunks = sub // chunk
    w_tril = [jnp.where(tril, ws_ref[g], 0.0).astype(BF16) for g in range(n_groups)]
    for r, (u, v) in enumerate(uvs):
        vb = v.astype(BF16)
        for g in range(n_groups):
            gs = slice(g * gw, (g + 1) * gw)
            rhs = jnp.concatenate([vb[c * chunk:(c + 1) * chunk, gs] for c in range(n_chunks)],
                                  axis=1)
            s_all = jnp.dot(w_tril[g], rhs, preferred_element_type=F32)
            for c in range(n_chunks):
                rs = slice(c * chunk, (c + 1) * chunk)
                s = s_all[:, c * gw:(c + 1) * gw] + bias[:, gs]
                m_ref[r * sub + c * chunk:r * sub + (c + 1) * chunk, gs] = (
                    u[rs, gs] * s).astype(m_ref.dtype)


def _gmlp_sample_kernel(x_ref, gmix_ref, win_ref, gv_ref, ws_ref, bias_ref, m_ref, v_ref, *, n_t):
    u, v = _gmlp_uv(x_ref[...], gmix_ref[...], win_ref[...], gv_ref[...])
    v_ref[...] = v
    db = x_ref.shape[0] // n_t
    vb = v.astype(BF16).astype(F32)
    for t in range(n_t):
        s = bias_ref[t:t + 1, :]
        for k in range(t + 1):
            w = ws_ref[t, k:k + 1, :].astype(BF16).astype(F32)
            s = s + w * vb[k * db:(k + 1) * db, :]
        m_ref[t * db:(t + 1) * db, :] = (u[t * db:(t + 1) * db, :] * s).astype(m_ref.dtype)


def _post_kernel(*refs, n_mix, ff_chunk, final, n_sub):
    h_ref = refs[0]
    mix_refs = refs[1:1 + n_mix]
    (wout_ref, p_ref, gffn_ref, w1_ref, w2_ref, gple_ref, wg_ref, wple_ref, gf_ref,
     out_ref) = refs[1 + n_mix:]
    sub = h_ref.shape[0] // n_sub
    groups = [slice(r * sub, (r + 1) * sub) for r in range(n_sub)]

    hs = []
    for rs in groups:
        h = h_ref[rs, :]
        row = 0
        for m_ref in mix_refs:
            h = h + jnp.dot(m_ref[rs, :], wout_ref[row:row + m_ref.shape[1], :],
                            preferred_element_type=F32)
            row += m_ref.shape[1]
        hs.append(h)
    acts = [_rms(h, gffn_ref[...]).astype(BF16) for h in hs]
    ys = [None] * n_sub
    for c in range(w1_ref.shape[1] // ff_chunk):
        cs = slice(c * ff_chunk, (c + 1) * ff_chunk)
        for r in range(n_sub):
            hid = jnp.maximum(jnp.dot(acts[r], w1_ref[:, cs], preferred_element_type=F32), 0.0)
            part = jnp.dot((hid * hid).astype(BF16), w2_ref[cs, :], preferred_element_type=F32)
            ys[r] = part if ys[r] is None else ys[r] + part
    hs = [h + y for h, y in zip(hs, ys)]
    acts = [_rms(h, gple_ref[...]).astype(BF16) for h in hs]
    gates = [jax.nn.sigmoid(jnp.dot(a, wg_ref[...], preferred_element_type=F32)) for a in acts]
    for rs, h, gate in zip(groups, hs, gates):
        pe = jnp.dot(p_ref[rs, :].astype(BF16), wple_ref[...], preferred_element_type=F32)
        h = h + gate * pe
        if final:
            h = _rms(h, gf_ref[...])
        out_ref[rs, :] = h


def _row_spec(tm, width):
    return pl.BlockSpec((tm, width), lambda i, *_: (i, 0))


def _ab_common_specs(tm, d, tab_width, tab_map, w):
    return [
        _row_spec(tm, d),
        pl.BlockSpec((tm, tab_width), tab_map),
        _const_spec((1, d)),
        _const_spec(w['w_in'].shape),
        _const_spec(w['g_q'].shape),
        _const_spec(w['g_kv'].shape),
        _const_spec(w['conv_w'].shape),
    ]


def _ab_prompt(h, tab, tab_t, w, batch, seq, tm):
    n, d = h.shape
    tiles_per_seq = seq // tm
    conv_w = w['conv_w'].shape[1]
    kv_rank = w['g_kv'].shape[1]
    rope = w['rope_dim']
    qw = w['w_qt'].shape[0]
    vw = w['w_vt'].shape[0]
    in_specs = _ab_common_specs(tm, d, LANES, lambda i: (i % tiles_per_seq, 0), w) + [
        pl.BlockSpec((rope, tm), lambda i: (0, i % tiles_per_seq)),
        _const_spec(w['w_qt'].shape), _const_spec(w['w_k'].shape), _const_spec(w['w_vt'].shape)]
    out_shape = (
        jax.ShapeDtypeStruct((n, conv_w), BF16),
        jax.ShapeDtypeStruct((n, kv_rank), F32),
        jax.ShapeDtypeStruct((n, rope), F32),
        jax.ShapeDtypeStruct((batch, qw, seq), BF16),
        jax.ShapeDtypeStruct((n, qw), BF16),
        jax.ShapeDtypeStruct((batch, vw, seq), BF16),
        jax.ShapeDtypeStruct((batch, SUBLANES, conv_w), F32),
    )

    def feature_major(width):
        return pl.BlockSpec((1, width, tm), lambda i: (i // tiles_per_seq, 0, i % tiles_per_seq))

    out_specs = (
        _row_spec(tm, conv_w), _row_spec(tm, kv_rank), _row_spec(tm, rope),
        feature_major(qw), _row_spec(tm, qw), feature_major(vw),
        pl.BlockSpec((1, SUBLANES, conv_w), lambda i: (i // tiles_per_seq, 0, 0)),
    )
    return pl.pallas_call(
        functools.partial(_ab_prompt_kernel, tiles_per_seq=tiles_per_seq, nope=w['nope']),
        grid=(n // tm,), in_specs=in_specs, out_specs=out_specs, out_shape=out_shape,
        scratch_shapes=[pltpu.VMEM((SUBLANES, conv_w), F32)],
        compiler_params=_params(("arbitrary",)), name="ab_prompt",
    )(h, tab, w['g_mix'], w['w_in'], w['g_q'], w['g_kv'], w['conv_w'],
      tab_t, w['w_qt'], w['w_k'], w['w_vt'])


def _ab_sample(h, tab, w, state, n_t):
    n, d = h.shape
    db = n // n_t
    conv_w = w['conv_w'].shape[1]
    kv_rank = w['g_kv'].shape[1]
    rope = w['rope_dim']
    qw = w['w_q'].shape[1] // 2
    n_heads = w['w_ukt'].shape[0]
    in_specs = _ab_common_specs(db, d, 3 * LANES, lambda i: (i, 0), w) + [
        _const_spec(w['w_q'].shape), _const_spec(w['w_ukt'].shape), _const_spec(state.shape)]
    out_shape = (
        jax.ShapeDtypeStruct((n, conv_w), BF16),
        jax.ShapeDtypeStruct((n, kv_rank), F32),
        jax.ShapeDtypeStruct((n, rope), F32),
        jax.ShapeDtypeStruct((n, qw), BF16),
        jax.ShapeDtypeStruct((n_heads, n, kv_rank), BF16),
        jax.ShapeDtypeStruct((n, conv_w), F32),
    )
    out_specs = (
        _row_spec(db, conv_w), _row_spec(db, kv_rank), _row_spec(db, rope), _row_spec(db, qw),
        pl.BlockSpec((n_heads, db, kv_rank), lambda i: (0, i, 0)),
        _row_spec(db, conv_w),
    )
    return pl.pallas_call(
        _ab_sample_kernel,
        grid=(n_t,), in_specs=in_specs, out_specs=out_specs, out_shape=out_shape,
        scratch_shapes=[pltpu.VMEM((db, conv_w), F32), pltpu.VMEM((db, conv_w), F32)],
        compiler_params=_params(("arbitrary",)), name="ab_sample",
    )(h, tab, w['g_mix'], w['w_in'], w['g_q'], w['g_kv'], w['conv_w'],
      w['w_q'], w['w_ukt'], state)


def _flash(qt, k, vt, scale, n_heads, tq):
    batch, seq, qw = k.shape
    vw = vt.shape[1]
    v_dim = vw // n_heads
    n_q = FLASH_Q_TILES if (seq // tq) % FLASH_Q_TILES == 0 else 1
    bq = n_q * tq
    pairs = [(i, j) for i in range(seq // bq) for j in range((i + 1) * n_q)]
    qi = jnp.asarray([p[0] for p in pairs], jnp.int32)
    kj = jnp.asarray([p[1] for p in pairs], jnp.int32)
    grid_spec = pltpu.PrefetchScalarGridSpec(
        num_scalar_prefetch=2,
        grid=(batch, len(pairs)),
        in_specs=[
            pl.BlockSpec((1, qw, bq), lambda b, t, qi, kj: (b, 0, qi[t])),
            pl.BlockSpec((1, tq, qw), lambda b, t, qi, kj: (b, kj[t], 0)),
            pl.BlockSpec((1, vw, tq), lambda b, t, qi, kj: (b, 0, kj[t])),
        ],
        out_specs=pl.BlockSpec((1, bq, vw), lambda b, t, qi, kj: (b, qi[t], 0)),
        scratch_shapes=[pltpu.VMEM((n_heads, 1, bq), F32),
                        pltpu.VMEM((n_heads, v_dim + SUM_ROWS, bq), F32)],
    )
    return pl.pallas_call(
        functools.partial(_flash_kernel, c_exp=scale * LOG2_E, n_heads=n_heads, v_dim=v_dim),
        grid_spec=grid_spec,
        out_shape=jax.ShapeDtypeStruct((batch, seq, vw), BF16),
        compiler_params=_params(("arbitrary", "arbitrary")), name="flash_prompt",
    )(qi, kj, qt, k, vt)


def _paged(page_table, ql, qp, kn, pn, w_uv, pool_c, pool_pt, scale, n_heads):
    db, rows, kv_rank = ql.shape
    rope = qp.shape[2]
    n_pages = page_table.shape[1]
    page = pool_c.shape[1]
    vw = w_uv.shape[1]
    cp = min(PAGES_PER_CHUNK, n_pages)
    piece = min(PAGES_PER_PIECE, cp)
    assert n_pages % cp == 0 and cp % piece == 0
    per_step = PAGED_SEQS_PER_STEP if db % PAGED_SEQS_PER_STEP == 0 else 1

    def seq_block(r, c):
        return pl.BlockSpec((per_step, r, c), lambda b, pt: (b, 0, 0))

    grid_spec = pltpu.PrefetchScalarGridSpec(
        num_scalar_prefetch=1, grid=(db // per_step,),
        in_specs=[seq_block(rows, kv_rank), seq_block(rows, rope), seq_block(SUBLANES, kv_rank),
                  seq_block(SUBLANES, rope),
                  pl.BlockSpec(memory_space=pl.ANY), pl.BlockSpec(memory_space=pl.ANY)],
        out_specs=seq_block(rows, kv_rank),
        scratch_shapes=[pltpu.VMEM((N_SLOTS, cp, page, kv_rank), F32),
                        pltpu.VMEM((N_SLOTS, rope, cp * page), F32),
                        pltpu.SemaphoreType.DMA((N_SLOTS,)),
                        pltpu.SemaphoreType.DMA((N_SLOTS,))])
    o_lat = pl.pallas_call(
        functools.partial(_paged_kernel, c_exp=scale * LOG2_E, n_pages=n_pages, n_heads=n_heads,
                          piece=piece),
        grid_spec=grid_spec, out_shape=jax.ShapeDtypeStruct((db, rows, kv_rank), BF16),
        compiler_params=_params(("arbitrary",)), name="paged_attn",
    )(page_table.reshape(-1), ql, qp, kn, pn, pool_c, pool_pt)
    n_tok = db * rows // n_heads
    return pl.pallas_call(
        functools.partial(_latent_out_kernel, n_heads=n_heads, v_dim=vw // n_heads),
        grid=(1,),
        in_specs=[_const_spec((db * rows, kv_rank)), _const_spec(w_uv.shape)],
        out_specs=_const_spec((n_tok, vw)),
        out_shape=jax.ShapeDtypeStruct((n_tok, vw), BF16),
        compiler_params=_params(("arbitrary",)), name="latent_out",
    )(o_lat.reshape(db * rows, kv_rank), w_uv)


def _gmlp_prompt(h, w, tm, chunk):
    n, d = h.shape
    width = w['w_in_c'].shape[1] // 2
    return pl.pallas_call(
        functools.partial(_gmlp_prompt_kernel, chunk=chunk, n_sub=max(1, tm // POST_GROUP_ROWS)),
        grid=(n // tm,),
        in_specs=[_row_spec(tm, d), _const_spec((1, d)), _const_spec(w['w_in_c'].shape),
                  _const_spec((1, width)), _const_spec(w['w_s'].shape),
                  _const_spec(w['bias_full'].shape)],
        out_specs=_row_spec(tm, width),
        out_shape=jax.ShapeDtypeStruct((n, width), BF16),
        compiler_params=_params(("arbitrary",)), name="gmlp_prompt",
    )(h, w['g_mix'], w['w_in_c'], w['g_v'], w['w_s'], w['bias_full'])


def _gmlp_sample(h, w, n_t):
    n, d = h.shape
    width = w['w_in_c'].shape[1] // 2
    return pl.pallas_call(
        functools.partial(_gmlp_sample_kernel, n_t=n_t),
        grid=(1,),
        in_specs=[_const_spec((n, d)), _const_spec((1, d)), _const_spec(w['w_in_c'].shape),
                  _const_spec((1, width)), _const_spec(w['ws_small'].shape),
                  _const_spec(w['bias_small'].shape)],
        out_specs=(_const_spec((n, width)), _const_spec((n, width))),
        out_shape=(jax.ShapeDtypeStruct((n, width), BF16), jax.ShapeDtypeStruct((n, width), F32)),
        compiler_params=_params(("arbitrary",)), name="gmlp_sample",
    )(h, w['g_mix'], w['w_in_c'], w['g_v'], w['ws_small'], w['bias_small'])


def _post(h, mixes, w_out, p, layer, w, tm, final, name):
    n, d = h.shape
    in_specs = ([_row_spec(tm, d)] + [_row_spec(tm, m.shape[1]) for m in mixes]
                + [_const_spec(w_out.shape),
                   pl.BlockSpec((None, tm, p.shape[2]), lambda i: (layer, i, 0)),
                   _const_spec((1, d)), _const_spec(w['w1'].shape),
                   _const_spec(w['w2'].shape), _const_spec((1, d)), _const_spec(w['w_gate'].shape),
                   _const_spec(w['w_ple'].shape), _const_spec((1, d))])
    return pl.pallas_call(
        functools.partial(_post_kernel, n_mix=len(mixes), ff_chunk=min(FF_CHUNK, w['w1'].shape[1]),
                          final=final, n_sub=max(1, tm // POST_GROUP_ROWS)),
        grid=(n // tm,), in_specs=in_specs, out_specs=_row_spec(tm, d),
        out_shape=jax.ShapeDtypeStruct((n, d), F32),
        compiler_params=_params(("arbitrary",)), name=name,
    )(h, *mixes, w_out, p, w['g_ffn'], w['w1'], w['w2'], w['g_ple'], w['w_gate'], w['w_ple'],
      w['g_f'])


def _rope_table(pos, rope_dim):
    inv = ROPE_THETA ** (-jnp.arange(0, rope_dim, 2, dtype=F32) / rope_dim)
    ang = pos.astype(F32)[:, None] * inv[None, :]
    c, s = jnp.cos(ang), jnp.sin(ang)
    t = pos.shape[0]
    cc = jnp.concatenate([c, c], axis=1)
    ss = jnp.concatenate([-s, s], axis=1)
    zeros = lambda w_: jnp.zeros((t, w_), F32)
    nope = LANES - 2 * rope_dim
    key_tab = jnp.concatenate([cc, ss, zeros(LANES - 2 * rope_dim)], axis=1)
    q_cos = jnp.concatenate([jnp.ones((t, nope), F32), cc, zeros(LANES - nope - rope_dim)], axis=1)
    q_sin = jnp.concatenate([zeros(nope), ss, zeros(LANES - nope - rope_dim)], axis=1)
    return jnp.concatenate([key_tab, q_cos, q_sin], axis=1)


def _prep_layer_ab(j, norm_mix_i, w_in_ab, conv_w, q_norm, w_uq, kv_norm, w_uk, w_uv):
    d = w_in_ab.shape[1]
    kv_rank, n_heads, nope = w_uk.shape[1:]
    rope = w_uq.shape[2] // n_heads - nope
    v_dim = w_uv.shape[3]
    half = rope // 2
    in_ab = w_in_ab.shape[2]
    w_in = w_in_ab[j]
    kpe0 = in_ab - rope
    pad = (-(in_ab + rope)) % (2 * LANES)
    w_in_p = jnp.concatenate(
        [w_in, w_in[:, kpe0 + half:], w_in[:, kpe0:kpe0 + half], jnp.zeros((d, pad), F32)], axis=1)
    q_rank = w_uq.shape[1]
    uq = w_uq[j].reshape(q_rank, n_heads, nope + rope)
    zpad = jnp.zeros((q_rank, n_heads, LANES - nope - rope), F32)
    wq_a = jnp.concatenate([uq, zpad], axis=2)
    wq_b = jnp.concatenate([jnp.zeros((q_rank, n_heads, nope), F32), uq[:, :, nope + half:],
                            uq[:, :, nope:nope + half], zpad], axis=2)
    w_q = jnp.concatenate([wq_a.reshape(q_rank, -1), wq_b.reshape(q_rank, -1)], axis=1)
    uk = w_uk[j]
    w_k = jnp.concatenate([uk, jnp.zeros((kv_rank, n_heads, LANES - nope), F32)], axis=2)
    w_ukt = jnp.concatenate([uk.transpose(1, 2, 0),
                             jnp.zeros((n_heads, LANES - nope, kv_rank), F32)], axis=1)
    return dict(
        g_mix=norm_mix_i.reshape(1, -1), w_in=w_in_p.astype(BF16), g_q=q_norm[j].reshape(1, -1),
        w_q=w_q.astype(BF16), w_qt=wq_a.reshape(q_rank, -1).T.astype(BF16),
        g_kv=kv_norm[j].reshape(1, -1), conv_w=conv_w[j],
        w_k=w_k.reshape(kv_rank, -1).astype(BF16),
        w_v=w_uv[j].reshape(kv_rank, n_heads * v_dim).astype(BF16),
        w_vt=w_uv[j].reshape(kv_rank, n_heads * v_dim).T.astype(BF16), w_ukt=w_ukt.astype(BF16),
        rope_dim=rope, n_heads=n_heads, nope=nope)


def _prep_layer_c(j, norm_mix_i, w_in_c, v_norm, w_s, b_s, n_t):
    n_groups, chunk = w_s.shape[1], w_s.shape[2]
    width = w_in_c.shape[2] // 2
    gw = width // n_groups
    bias_full = jnp.repeat(b_s[j].T, gw, axis=1)
    ws_small = jnp.repeat(w_s[j][:, :n_t, :n_t].transpose(1, 2, 0), gw, axis=2)
    return dict(g_mix=norm_mix_i.reshape(1, -1), w_in_c=w_in_c[j].astype(BF16),
                g_v=v_norm[j].reshape(1, -1), w_s=w_s[j], bias_full=bias_full,
                ws_small=ws_small, bias_small=bias_full[:n_t], chunk=chunk)


def _prep_post(i, norm_ffn, w_ff1, w_ff2, norm_ple, w_ple_gate, w_ple, norm_f):
    return dict(g_ffn=norm_ffn[i].reshape(1, -1), w1=w_ff1[i].astype(BF16), w2=w_ff2[i].astype(BF16),
                g_ple=norm_ple[i].reshape(1, -1), w_gate=w_ple_gate[i].astype(BF16),
                w_ple=w_ple[i].astype(BF16), g_f=norm_f.reshape(1, -1))


def _token_tile(n, cap):
    tm = min(cap, n)
    while n % tm:
        tm //= 2
    return tm


def kernel(x_prompt, x_sample, cache_ckv, cache_kpe, state_conv, page_table, p_prompt, p_sample, norm_mix, w_in_ab, conv_w, q_norm, w_uq, kv_norm, w_uk, w_uv, w_out_ab, w_in_c, v_norm, w_s, b_s, w_out_c, norm_ffn, w_ff1, w_ff2, norm_ple, w_ple_gate, w_ple, norm_f):
    depth = norm_mix.shape[0]
    batch, seq, d = x_prompt.shape
    db, n_t, _ = x_sample.shape
    n_pages = page_table.shape[1]
    page = cache_ckv.shape[2]
    past_len = n_pages * page
    n_heads, nope = w_uk.shape[2], w_uk.shape[3]
    rope = cache_kpe.shape[3]
    scale = float(nope + rope) ** -0.5

    tab_p = _rope_table(jnp.arange(seq, dtype=jnp.int32), rope)
    tab_s = jnp.repeat(_rope_table(past_len + jnp.arange(n_t, dtype=jnp.int32), rope), db, axis=0)

    n_p = batch * seq
    n_s = db * n_t
    tm_p = _token_tile(seq, 512)
    tm_post = _token_tile(seq, POST_TILE_ROWS)
    tq = _token_tile(seq, 512)
    hp = x_prompt.reshape(n_p, d)
    hs = x_sample.transpose(1, 0, 2).reshape(n_s, d)
    pp = p_prompt.reshape(depth, n_p, -1)
    ps = p_sample.transpose(0, 2, 1, 3).reshape(depth, n_s, -1)

    def unmajor(a):
        return a.reshape(n_t, db, a.shape[-1]).transpose(1, 0, 2)

    conv_p, conv_s, ckv_p, kpe_p, ckv_s, kpe_s, v_s = [], [], [], [], [], [], []
    for i in range(depth):
        j = i // 2
        wpost = _prep_post(i, norm_ffn, w_ff1, w_ff2, norm_ple, w_ple_gate, w_ple, norm_f)
        final = i == depth - 1
        if i % 2 == 0:
            w = _prep_layer_ab(j, norm_mix[i], w_in_ab, conv_w, q_norm, w_uq, kv_norm, w_uk, w_uv)
            w_out = w_out_ab[j].astype(BF16)
            hr = rope // 2
            cos_sin_t = jnp.concatenate([tab_p[:, :hr], tab_p[:, rope + hr:2 * rope]], axis=1).T
            yconv, ckv, kpe, qt, k, vt, cstate = _ab_prompt(
                hp, tab_p[:, :LANES], cos_sin_t, w, batch, seq, tm_p)
            yatt_p = _flash(qt, k.reshape(batch, seq, -1), vt, scale, n_heads, tq)
            conv_p.append(cstate[:, SUBLANES - 2:, :])
            ckv_p.append(ckv.reshape(batch, seq, -1))
            kpe_p.append(kpe.reshape(batch, seq, -1))
            state = state_conv[j].transpose(1, 0, 2)
            yconv_s, ckv, kpe, q, qlat, up = _ab_sample(hs, tab_s, w, state, n_t)
            kv_rank = ckv.shape[1]
            ql = qlat.reshape(n_heads, n_t, db, kv_rank).transpose(2, 1, 0, 3).reshape(
                db, n_t * n_heads, kv_rank)
            qp = q.reshape(n_t, db, n_heads, LANES)[..., nope:nope + rope].transpose(
                1, 0, 2, 3).reshape(db, n_t * n_heads, rope)
            ckv_b = unmajor(ckv)
            kpe_b = unmajor(kpe)
            padr = ((0, 0), (0, SUBLANES - n_t), (0, 0))
            hp = _post(hp, [yconv, yatt_p.reshape(n_p, -1)], w_out, pp, i, wpost, tm_post, final,
                       "post_prompt")
            yatt = _paged(page_table, ql, qp, jnp.pad(ckv_b, padr), jnp.pad(kpe_b, padr),
                          w['w_v'], cache_ckv[j], jnp.swapaxes(cache_kpe[j], 1, 2), scale, n_heads)
            yatt = yatt.reshape(db, n_t, -1).transpose(1, 0, 2).reshape(n_s, -1)
            hs = _post(hs, [yconv_s, yatt], w_out, ps, i, wpost, n_s, final, "post_sample")
            conv_s.append(unmajor(up)[:, n_t - 2:, :])
            ckv_s.append(ckv_b)
            kpe_s.append(kpe_b)
        else:
            w = _prep_layer_c(j, norm_mix[i], w_in_c, v_norm, w_s, b_s, n_t)
            w_out = w_out_c[j].astype(BF16)
            m = _gmlp_prompt(hp, w, tm_post, w['chunk'])
            hp = _post(hp, [m], w_out, pp, i, wpost, tm_post, final, "post_prompt")
            m, v = _gmlp_sample(hs, w, n_t)
            hs = _post(hs, [m], w_out, ps, i, wpost, n_s, final, "post_sample")
            v_s.append(unmajor(v))

    y_prompt = hp.reshape(batch, seq, d)
    y_sample = unmajor(hs)
    return (y_prompt, y_sample, jnp.stack(conv_p), jnp.stack(conv_s), jnp.stack(ckv_p),
            jnp.stack(kpe_p), jnp.stack(ckv_s), jnp.stack(kpe_s), jnp.stack(v_s))
```

```python
import functools
import math

import jax
import jax.numpy as jnp
from jax import lax
from jax.experimental import pallas as pl
from jax.experimental.pallas import tpu as pltpu

F32 = jnp.float32
BF16 = jnp.bfloat16

EPS = 1e-6
ROPE_THETA = 10000.0
LANES = 128
SUBLANES = 8
VMEM_LIMIT_BYTES = 62 * 1024 * 1024
PAGES_PER_CHUNK = 32
PAGES_PER_PIECE = 8
N_STREAMS = 2
N_SLOTS = 4
FF_CHUNK = 1024
POST_GROUP_ROWS = 512
POST_TILE_ROWS = 1024
PAGED_SEQS_PER_STEP = 2
FLASH_Q_TILES = 2
QK_LOOKAHEAD = 2
SUM_ROWS = 16
MASK_VALUE = -1e30
LOG2_E = math.log2(math.e)


def _rms(x, g):
    ms = jnp.mean(x * x, axis=-1, keepdims=True)
    return x * lax.rsqrt(ms + EPS) * g


def _const_spec(shape):
    nd = len(shape)
    return pl.BlockSpec(shape, lambda *_: (0,) * nd, pipeline_mode=pl.Buffered(1))


def _params(semantics):
    return pltpu.CompilerParams(dimension_semantics=semantics,
                                vmem_limit_bytes=VMEM_LIMIT_BYTES)


def _ab_project(x_ref, tab_ref, gmix_ref, win_ref, gq_ref, gkv_ref, cw_ref, rope):
    cw = cw_ref.shape[1]
    q_rank = gq_ref.shape[1]
    kv_rank = gkv_ref.shape[1]
    a = _rms(x_ref[...], gmix_ref[...]).astype(BF16)
    z = jnp.dot(a, win_ref[...], preferred_element_type=F32)
    o_q = 3 * cw
    o_kv = o_q + q_rank
    o_pe = o_kv + kv_rank
    up = z[:, cw:2 * cw] * z[:, 0:cw]
    gb = z[:, 2 * cw:o_q]
    ckv_n = _rms(z[:, o_kv:o_pe], gkv_ref[...])
    prod = z[:, o_pe:o_pe + LANES] * tab_ref[:, 0:LANES]
    kr = prod + pltpu.roll(prod, LANES - rope, axis=1)
    qn = _rms(z[:, o_q:o_kv], gq_ref[...]).astype(BF16)
    return up, gb, ckv_n, kr, qn


NT_DIMS = (((1,), (1,)), ((), ()))


def _ab_prompt_kernel(x_ref, tab_ref, gmix_ref, win_ref, gq_ref, gkv_ref, cw_ref,
                      tabt_ref, wqt_ref, wk_ref, wvt_ref,
                      yconv_ref, ckv_ref, kpe_ref, qt_ref, k_ref, vt_ref, cstate_ref,
                      carry_ref, *, tiles_per_seq, nope):
    i = pl.program_id(0)

    @pl.when(i % tiles_per_seq == 0)
    def _():
        carry_ref[...] = jnp.zeros_like(carry_ref)

    up, gb, ckv_n, kr, qn = _ab_project(
        x_ref, tab_ref, gmix_ref, win_ref, gq_ref, gkv_ref, cw_ref, kpe_ref.shape[1])
    tm = up.shape[0]
    c0 = carry_ref[SUBLANES - 2:SUBLANES - 1, :]
    c1 = carry_ref[SUBLANES - 1:SUBLANES, :]
    row = lax.broadcasted_iota(jnp.int32, (tm, 1), 0)
    um1 = jnp.where(row == 0, c1, pltpu.roll(up, 1, axis=0))
    um2 = jnp.where(row == 0, c0, jnp.where(row == 1, c1, pltpu.roll(up, 2, axis=0)))
    cw = cw_ref[...]
    conv = cw[0:1, :] * um2 + cw[1:2, :] * um1 + cw[2:3, :] * up
    yconv_ref[...] = (gb * conv).astype(yconv_ref.dtype)
    tail = up[tm - SUBLANES:, :]
    carry_ref[...] = tail
    cstate_ref[0] = tail

    ckv_ref[...] = ckv_n
    kpe_ref[...] = kr[:, 0:kpe_ref.shape[1]]
    rope = kpe_ref.shape[1]
    hr = rope // 2
    qt = lax.dot_general(wqt_ref[...], qn, NT_DIMS, preferred_element_type=F32)
    cos_t = tabt_ref[0:hr, :]
    sin_t = tabt_ref[hr:rope, :]
    pieces = []
    for h in range(qt.shape[0] // LANES):
        base = h * LANES
        x1 = qt[base + nope:base + nope + hr]
        x2 = qt[base + nope + hr:base + nope + rope]
        pieces += [qt[base:base + nope], x1 * cos_t - x2 * sin_t, x2 * cos_t + x1 * sin_t,
                   qt[base + nope + rope:base + LANES]]
    qt_ref[0] = jnp.concatenate(pieces, axis=0).astype(qt_ref.dtype)
    ckv_b = ckv_n.astype(BF16)
    lane = lax.broadcasted_iota(jnp.int32, kr.shape, 1)
    k_rot = jnp.where((lane >= nope) & (lane < nope + rope), pltpu.roll(kr, nope, axis=1), 0.0)
    k_cat = (jnp.dot(ckv_b, wk_ref[...], preferred_element_type=F32)
             + jnp.concatenate([k_rot] * (wk_ref.shape[1] // LANES), axis=1))
    k_ref[...] = k_cat.astype(k_ref.dtype)
    vt_ref[0] = lax.dot_general(wvt_ref[...], ckv_b, NT_DIMS,
                                preferred_element_type=F32).astype(vt_ref.dtype)


def _ab_sample_kernel(x_ref, tab_ref, gmix_ref, win_ref, gq_ref, gkv_ref, cw_ref,
                      wq_ref, wukt_ref, state_ref,
                      yconv_ref, ckv_ref, kpe_ref, q_ref, qlat_ref, up_ref,
                      prev1_ref, prev2_ref):
    t = pl.program_id(0)

    @pl.when(t == 0)
    def _():
        prev2_ref[...] = state_ref[0]
        prev1_ref[...] = state_ref[1]

    up, gb, ckv_n, kr, qn = _ab_project(
        x_ref, tab_ref, gmix_ref, win_ref, gq_ref, gkv_ref, cw_ref, kpe_ref.shape[1])
    q2 = jnp.dot(qn, wq_ref[...], preferred_element_type=F32)
    half = q2.shape[1] // 2
    cpat = jnp.concatenate([tab_ref[:, LANES:2 * LANES]] * (half // LANES), axis=1)
    spat = jnp.concatenate([tab_ref[:, 2 * LANES:3 * LANES]] * (half // LANES), axis=1)
    q_r = q2[:, :half] * cpat + q2[:, half:] * spat
    cw = cw_ref[...]
    prev1 = prev1_ref[...]
    conv = cw[0:1, :] * prev2_ref[...] + cw[1:2, :] * prev1 + cw[2:3, :] * up
    yconv_ref[...] = (gb * conv).astype(yconv_ref.dtype)
    up_ref[...] = up
    prev2_ref[...] = prev1
    prev1_ref[...] = up

    ckv_ref[...] = ckv_n
    kpe_ref[...] = kr[:, 0:kpe_ref.shape[1]]
    q_b = q_r.astype(BF16)
    q_ref[...] = q_b
    for h in range(qlat_ref.shape[0]):
        qlat_ref[h] = jnp.dot(q_b[:, h * LANES:(h + 1) * LANES], wukt_ref[h],
                              preferred_element_type=F32).astype(qlat_ref.dtype)


def _flash_kernel(qi_ref, kj_ref, qt_ref, k_ref, vt_ref, o_ref, m_ref, acc_ref, *,
                  c_exp, n_heads, v_dim):
    t = pl.program_id(1)
    qi = qi_ref[t]
    kj = kj_ref[t]
    tk = k_ref.shape[1]
    tq = tk
    n_q = qt_ref.shape[2] // tq

    @pl.when(kj == 0)
    def _():
        m_ref[...] = jnp.full_like(m_ref, MASK_VALUE)
        acc_ref[...] = jnp.zeros_like(acc_ref)

    def step(masked, qs):
        ones = jnp.ones((SUM_ROWS, tk), BF16)
        if masked:
            k_id = lax.broadcasted_iota(jnp.int32, (tk, tq), 0)
            q_id = lax.broadcasted_iota(jnp.int32, (tk, tq), 1)
            keep = k_id <= q_id
        def scores_t(h):
            k = k_ref[0, :, h * LANES:(h + 1) * LANES]
            qt = qt_ref[0, h * LANES:(h + 1) * LANES, qs]
            return jnp.dot(k, qt, preferred_element_type=F32)

        pending = [scores_t(h) for h in range(min(QK_LOOKAHEAD, n_heads))]
        for h in range(n_heads):
            st = pending.pop(0)
            if h + QK_LOOKAHEAD < n_heads:
                pending.append(scores_t(h + QK_LOOKAHEAD))
            if masked:
                st = jnp.where(keep, st, MASK_VALUE)
            m_old = m_ref[h, :, qs]
            m_new = jnp.maximum(m_old, jnp.max(st, axis=0, keepdims=True))
            alpha = jnp.exp2((m_old - m_new) * c_exp)
            pt = jnp.exp2((st - m_new) * c_exp).astype(BF16)
            vx = jnp.concatenate([vt_ref[0, h * v_dim:(h + 1) * v_dim, :], ones], axis=0)
            acc_ref[h, :, qs] = (alpha * acc_ref[h, :, qs]
                                 + jnp.dot(vx, pt, preferred_element_type=F32))
            m_ref[h, :, qs] = m_new

    for s in range(n_q):
        qs = slice(s * tq, (s + 1) * tq)
        q_tile = qi * n_q + s

        @pl.when(kj < q_tile)
        def _(qs=qs):
            step(False, qs)

        @pl.when(kj == q_tile)
        def _(qs=qs):
            step(True, qs)
            ot = jnp.concatenate(
                [acc_ref[h, 0:v_dim, qs] / acc_ref[h, v_dim:v_dim + 1, qs]
                 for h in range(n_heads)], axis=0)
            o_ref[0, qs, :] = ot.T.astype(o_ref.dtype)


def _paged_kernel(pt_ref, ql_ref, *refs, **static):
    per_step = ql_ref.shape[0]
    for e in range(per_step):
        _paged_sequence(pl.program_id(0) * per_step + e, pl.num_programs(0) * per_step, e,
                        pt_ref, ql_ref, *refs, **static)


def _paged_sequence(b, n_b, e, pt_ref, ql_ref, qp_ref, kn_ref, pn_ref, pool_c, pool_pt,
                    o_ref, cbuf, pbuf, sem_c, sem_p, *, c_exp, n_pages, n_heads, piece):
    cp = cbuf.shape[1]
    page = cbuf.shape[2]
    n_chunks = n_pages // cp

    def page_copies(pg, k, slot):
        return (pltpu.make_async_copy(pool_c.at[pg], cbuf.at[slot, k], sem_c.at[slot]),
                pltpu.make_async_copy(pool_pt.at[pg], pbuf.at[slot, :, pl.ds(k * page, page)],
                                      sem_p.at[slot]))

    def start_chunk(g, slot):
        for k in range(cp):
            for cpy in page_copies(pt_ref[g * cp + k], k, slot):
                cpy.start()

    def wait_chunk(slot):
        for k in range(cp):
            for cpy in page_copies(0, k, slot):
                cpy.wait()

    last_g = n_b * n_chunks - 1
    n_slots = cbuf.shape[0]
    ahead = n_slots - 1

    @pl.when(b == 0)
    def _():
        for a in range(ahead):
            start_chunk(jnp.minimum(a, last_g), a)

    ql = ql_ref[e]
    qp = qp_ref[e]

    kn = kn_ref[e].astype(BF16)
    pn = pn_ref[e].astype(BF16)
    s0 = (lax.dot_general(ql, kn, NT_DIMS, preferred_element_type=F32)
          + lax.dot_general(qp, pn, NT_DIMS, preferred_element_type=F32))
    r_id = lax.broadcasted_iota(jnp.int32, s0.shape, 0)
    c_id = lax.broadcasted_iota(jnp.int32, s0.shape, 1)
    s0 = jnp.where(c_id <= r_id // n_heads, s0, MASK_VALUE)
    m0 = jnp.max(s0, axis=1, keepdims=True)
    p0 = jnp.exp2((s0 - m0) * c_exp)
    l0 = jnp.sum(p0, axis=1, keepdims=True)
    acc0 = jnp.dot(p0.astype(BF16), kn, preferred_element_type=F32)

    def latent_piece(u, slot):
        kc = cbuf[slot, u * piece:(u + 1) * piece].reshape(piece * page, cbuf.shape[3])
        return kc.astype(BF16)

    def scores(u, slot):
        kpt = pbuf[slot, :, u * piece * page:(u + 1) * piece * page].astype(BF16)
        return (lax.dot_general(ql, latent_piece(u, slot), NT_DIMS, preferred_element_type=F32)
                + jnp.dot(qp, kpt, preferred_element_type=F32))

    def absorb(state, s, u, slot):
        m_run, l_run, acc = state
        m_new = jnp.maximum(m_run, jnp.max(s, axis=1, keepdims=True))
        alpha = jnp.exp2((m_run - m_new) * c_exp)
        p = jnp.exp2((s - m_new) * c_exp)
        l_new = alpha * l_run + jnp.sum(p, axis=1, keepdims=True)
        acc = alpha * acc + jnp.dot(p.astype(BF16), latent_piece(u, slot),
                                    preferred_element_type=F32)
        return m_new, l_new, acc

    def body(c, streams):
        g = b * n_chunks + c
        slot = g % n_slots
        wait_chunk(slot)
        start_chunk(jnp.minimum(g + ahead, last_g), (g + ahead) % n_slots)
        streams = list(streams)
        s_all = [scores(u, slot) for u in range(cp // piece)]
        for u, s in enumerate(s_all):
            streams[u % N_STREAMS] = absorb(streams[u % N_STREAMS], s, u, slot)
        return tuple(streams)

    empty = (jnp.full_like(m0, MASK_VALUE), jnp.zeros_like(l0), jnp.zeros_like(acc0))
    streams = lax.fori_loop(0, n_chunks, body, ((m0, l0, acc0),) + (empty,) * (N_STREAMS - 1))

    @pl.when(b == n_b - 1)
    def _():
        for a in range(1, n_slots):
            wait_chunk((last_g + a) % n_slots)

    m_fin = functools.reduce(jnp.maximum, [st[0] for st in streams])
    weights = [jnp.exp2((st[0] - m_fin) * c_exp) for st in streams]
    l_fin = sum(wt * st[1] for wt, st in zip(weights, streams))
    acc = sum(wt * st[2] for wt, st in zip(weights, streams))
    o_ref[e] = (acc / l_fin).astype(o_ref.dtype)


def _latent_out_kernel(o_ref, wuv_ref, y_ref, *, n_heads, v_dim):
    full = jnp.dot(o_ref[...], wuv_ref[...], preferred_element_type=F32)
    r_id = lax.broadcasted_iota(jnp.int32, full.shape, 0)
    c_id = lax.broadcasted_iota(jnp.int32, full.shape, 1)
    full = jnp.where(c_id // v_dim == r_id % n_heads, full, 0.0)
    y_ref[...] = jnp.sum(full.reshape(full.shape[0] // n_heads, n_heads, full.shape[1]),
                         axis=1).astype(y_ref.dtype)


def _gelu(x):
    return 0.5 * x * (1.0 + lax.erf(x * (1.0 / math.sqrt(2.0))))


def _gmlp_uv(x, g_mix, w_in, g_v):
    a = _rms(x, g_mix).astype(BF16)
    z = _gelu(jnp.dot(a, w_in, preferred_element_type=F32))
    width = z.shape[1] // 2
    return z[:, :width], _rms(z[:, width:], g_v)


def _gmlp_prompt_kernel(x_ref, gmix_ref, win_ref, gv_ref, ws_ref, bias_ref, m_ref, *, chunk,
                        n_sub):
    sub = x_ref.shape[0] // n_sub
    uvs = [_gmlp_uv(x_ref[r * sub:(r + 1) * sub, :], gmix_ref[...], win_ref[...], gv_ref[...])
           for r in range(n_sub)]
    n_groups = ws_ref.shape[0]
    r_id = lax.broadcasted_iota(jnp.int32, (chunk, chunk), 0)
    c_id = lax.broadcasted_iota(jnp.int32, (chunk, chunk), 1)
    tril = c_id <= r_id
    bias = bias_ref[...]
    gw = m_ref.shape[1] // n_groups
    n_chunks = sub // chunk
    w_tril = [jnp.where(tril, ws_ref[g], 0.0).astype(BF16) for g in range(n_groups)]
    for r, (u, v) in enumerate(uvs):
        vb = v.astype(BF16)
        for g in range(n_groups):
            gs = slice(g * gw, (g + 1) * gw)
            rhs = jnp.concatenate([vb[c * chunk:(c + 1) * chunk, gs] for c in range(n_chunks)],
                                  axis=1)
            s_all = jnp.dot(w_tril[g], rhs, preferred_element_type=F32)
            for c in range(n_chunks):
                rs = slice(c * chunk, (c + 1) * chunk)
                s = s_all[:, c * gw:(c + 1) * gw] + bias[:, gs]
                m_ref[r * sub + c * chunk:r * sub + (c + 1) * chunk, gs] = (
                    u[rs, gs] * s).astype(m_ref.dtype)


def _gmlp_sample_kernel(x_ref, gmix_ref, win_ref, gv_ref, ws_ref, bias_ref, m_ref, v_ref, *, n_t):
    u, v = _gmlp_uv(x_ref[...], gmix_ref[...], win_ref[...], gv_ref[...])
    v_ref[...] = v
    db = x_ref.shape[0] // n_t
    vb = v.astype(BF16).astype(F32)
    for t in range(n_t):
        s = bias_ref[t:t + 1, :]
        for k in range(t + 1):
            w = ws_ref[t, k:k + 1, :].astype(BF16).astype(F32)
            s = s + w * vb[k * db:(k + 1) * db, :]
        m_ref[t * db:(t + 1) * db, :] = (u[t * db:(t + 1) * db, :] * s).astype(m_ref.dtype)


def _post_kernel(*refs, n_mix, ff_chunk, final, n_sub):
    h_ref = refs[0]
    mix_refs = refs[1:1 + n_mix]
    (wout_ref, p_ref, gffn_ref, w1_ref, w2_ref, gple_ref, wg_ref, wple_ref, gf_ref,
     out_ref) = refs[1 + n_mix:]
    sub = h_ref.shape[0] // n_sub
    groups = [slice(r * sub, (r + 1) * sub) for r in range(n_sub)]

    hs = []
    for rs in groups:
        h = h_ref[rs, :]
        row = 0
        for m_ref in mix_refs:
            h = h + jnp.dot(m_ref[rs, :], wout_ref[row:row + m_ref.shape[1], :],
                            preferred_element_type=F32)
            row += m_ref.shape[1]
        hs.append(h)
    acts = [_rms(h, gffn_ref[...]).astype(BF16) for h in hs]
    ys = [None] * n_sub
    for c in range(w1_ref.shape[1] // ff_chunk):
        cs = slice(c * ff_chunk, (c + 1) * ff_chunk)
        for r in range(n_sub):
            hid = jnp.maximum(jnp.dot(acts[r], w1_ref[:, cs], preferred_element_type=F32), 0.0)
            part = jnp.dot((hid * hid).astype(BF16), w2_ref[cs, :], preferred_element_type=F32)
            ys[r] = part if ys[r] is None else ys[r] + part
    hs = [h + y for h, y in zip(hs, ys)]
    acts = [_rms(h, gple_ref[...]).astype(BF16) for h in hs]
    gates = [jax.nn.sigmoid(jnp.dot(a, wg_ref[...], preferred_element_type=F32)) for a in acts]
    for rs, h, gate in zip(groups, hs, gates):
        pe = jnp.dot(p_ref[rs, :].astype(BF16), wple_ref[...], preferred_element_type=F32)
        h = h + gate * pe
        if final:
            h = _rms(h, gf_ref[...])
        out_ref[rs, :] = h


def _row_spec(tm, width):
    return pl.BlockSpec((tm, width), lambda i, *_: (i, 0))


def _ab_common_specs(tm, d, tab_width, tab_map, w):
    return [
        _row_spec(tm, d),
        pl.BlockSpec((tm, tab_width), tab_map),
        _const_spec((1, d)),
        _const_spec(w['w_in'].shape),
        _const_spec(w['g_q'].shape),
        _const_spec(w['g_kv'].shape),
        _const_spec(w['conv_w'].shape),
    ]


def _ab_prompt(h, tab, tab_t, w, batch, seq, tm):
    n, d = h.shape
    tiles_per_seq = seq // tm
    conv_w = w['conv_w'].shape[1]
    kv_rank = w['g_kv'].shape[1]
    rope = w['rope_dim']
    qw = w['w_qt'].shape[0]
    vw = w['w_vt'].shape[0]
    in_specs = _ab_common_specs(tm, d, LANES, lambda i: (i % tiles_per_seq, 0), w) + [
        pl.BlockSpec((rope, tm), lambda i: (0, i % tiles_per_seq)),
        _const_spec(w['w_qt'].shape), _const_spec(w['w_k'].shape), _const_spec(w['w_vt'].shape)]
    out_shape = (
        jax.ShapeDtypeStruct((n, conv_w), BF16),
        jax.ShapeDtypeStruct((n, kv_rank), F32),
        jax.ShapeDtypeStruct((n, rope), F32),
        jax.ShapeDtypeStruct((batch, qw, seq), BF16),
        jax.ShapeDtypeStruct((n, qw), BF16),
        jax.ShapeDtypeStruct((batch, vw, seq), BF16),
        jax.ShapeDtypeStruct((batch, SUBLANES, conv_w), F32),
    )

    def feature_major(width):
        return pl.BlockSpec((1, width, tm), lambda i: (i // tiles_per_seq, 0, i % tiles_per_seq))

    out_specs = (
        _row_spec(tm, conv_w), _row_spec(tm, kv_rank), _row_spec(tm, rope),
        feature_major(qw), _row_spec(tm, qw), feature_major(vw),
        pl.BlockSpec((1, SUBLANES, conv_w), lambda i: (i // tiles_per_seq, 0, 0)),
    )
    return pl.pallas_call(
        functools.partial(_ab_prompt_kernel, tiles_per_seq=tiles_per_seq, nope=w['nope']),
        grid=(n // tm,), in_specs=in_specs, out_specs=out_specs, out_shape=out_shape,
        scratch_shapes=[pltpu.VMEM((SUBLANES, conv_w), F32)],
        compiler_params=_params(("arbitrary",)), name="ab_prompt",
    )(h, tab, w['g_mix'], w['w_in'], w['g_q'], w['g_kv'], w['conv_w'],
      tab_t, w['w_qt'], w['w_k'], w['w_vt'])


def _ab_sample(h, tab, w, state, n_t):
    n, d = h.shape
    db = n // n_t
    conv_w = w['conv_w'].shape[1]
    kv_rank = w['g_kv'].shape[1]
    rope = w['rope_dim']
    qw = w['w_q'].shape[1] // 2
    n_heads = w['w_ukt'].shape[0]
    in_specs = _ab_common_specs(db, d, 3 * LANES, lambda i: (i, 0), w) + [
        _const_spec(w['w_q'].shape), _const_spec(w['w_ukt'].shape), _const_spec(state.shape)]
    out_shape = (
        jax.ShapeDtypeStruct((n, conv_w), BF16),
        jax.ShapeDtypeStruct((n, kv_rank), F32),
        jax.ShapeDtypeStruct((n, rope), F32),
        jax.ShapeDtypeStruct((n, qw), BF16),
        jax.ShapeDtypeStruct((n_heads, n, kv_rank), BF16),
        jax.ShapeDtypeStruct((n, conv_w), F32),
    )
    out_specs = (
        _row_spec(db, conv_w), _row_spec(db, kv_rank), _row_spec(db, rope), _row_spec(db, qw),
        pl.BlockSpec((n_heads, db, kv_rank), lambda i: (0, i, 0)),
        _row_spec(db, conv_w),
    )
    return pl.pallas_call(
        _ab_sample_kernel,
        grid=(n_t,), in_specs=in_specs, out_specs=out_specs, out_shape=out_shape,
        scratch_shapes=[pltpu.VMEM((db, conv_w), F32), pltpu.VMEM((db, conv_w), F32)],
        compiler_params=_params(("arbitrary",)), name="ab_sample",
    )(h, tab, w['g_mix'], w['w_in'], w['g_q'], w['g_kv'], w['conv_w'],
      w['w_q'], w['w_ukt'], state)


def _flash(qt, k, vt, scale, n_heads, tq):
    batch, seq, qw = k.shape
    vw = vt.shape[1]
    v_dim = vw // n_heads
    n_q = FLASH_Q_TILES if (seq // tq) % FLASH_Q_TILES == 0 else 1
    bq = n_q * tq
    pairs = [(i, j) for i in range(seq // bq) for j in range((i + 1) * n_q)]
    qi = jnp.asarray([p[0] for p in pairs], jnp.int32)
    kj = jnp.asarray([p[1] for p in pairs], jnp.int32)
    grid_spec = pltpu.PrefetchScalarGridSpec(
        num_scalar_prefetch=2,
        grid=(batch, len(pairs)),
        in_specs=[
            pl.BlockSpec((1, qw, bq), lambda b, t, qi, kj: (b, 0, qi[t])),
            pl.BlockSpec((1, tq, qw), lambda b, t, qi, kj: (b, kj[t], 0)),
            pl.BlockSpec((1, vw, tq), lambda b, t, qi, kj: (b, 0, kj[t])),
        ],
        out_specs=pl.BlockSpec((1, bq, vw), lambda b, t, qi, kj: (b, qi[t], 0)),
        scratch_shapes=[pltpu.VMEM((n_heads, 1, bq), F32),
                        pltpu.VMEM((n_heads, v_dim + SUM_ROWS, bq), F32)],
    )
    return pl.pallas_call(
        functools.partial(_flash_kernel, c_exp=scale * LOG2_E, n_heads=n_heads, v_dim=v_dim),
        grid_spec=grid_spec,
        out_shape=jax.ShapeDtypeStruct((batch, seq, vw), BF16),
        compiler_params=_params(("arbitrary", "arbitrary")), name="flash_prompt",
    )(qi, kj, qt, k, vt)


def _paged(page_table, ql, qp, kn, pn, w_uv, pool_c, pool_pt, scale, n_heads):
    db, rows, kv_rank = ql.shape
    rope = qp.shape[2]
    n_pages = page_table.shape[1]
    page = pool_c.shape[1]
    vw = w_uv.shape[1]
    cp = min(PAGES_PER_CHUNK, n_pages)
    piece = min(PAGES_PER_PIECE, cp)
    assert n_pages % cp == 0 and cp % piece == 0
    per_step = PAGED_SEQS_PER_STEP if db % PAGED_SEQS_PER_STEP == 0 else 1

    def seq_block(r, c):
        return pl.BlockSpec((per_step, r, c), lambda b, pt: (b, 0, 0))

    grid_spec = pltpu.PrefetchScalarGridSpec(
        num_scalar_prefetch=1, grid=(db // per_step,),
        in_specs=[seq_block(rows, kv_rank), seq_block(rows, rope), seq_block(SUBLANES, kv_rank),
                  seq_block(SUBLANES, rope),
                  pl.BlockSpec(memory_space=pl.ANY), pl.BlockSpec(memory_space=pl.ANY)],
        out_specs=seq_block(rows, kv_rank),
        scratch_shapes=[pltpu.VMEM((N_SLOTS, cp, page, kv_rank), F32),
                        pltpu.VMEM((N_SLOTS, rope, cp * page), F32),
                        pltpu.SemaphoreType.DMA((N_SLOTS,)),
                        pltpu.SemaphoreType.DMA((N_SLOTS,))])
    o_lat = pl.pallas_call(
        functools.partial(_paged_kernel, c_exp=scale * LOG2_E, n_pages=n_pages, n_heads=n_heads,
                          piece=piece),
        grid_spec=grid_spec, out_shape=jax.ShapeDtypeStruct((db, rows, kv_rank), BF16),
        compiler_params=_params(("arbitrary",)), name="paged_attn",
    )(page_table.reshape(-1), ql, qp, kn, pn, pool_c, pool_pt)
    n_tok = db * rows // n_heads
    return pl.pallas_call(
        functools.partial(_latent_out_kernel, n_heads=n_heads, v_dim=vw // n_heads),
        grid=(1,),
        in_specs=[_const_spec((db * rows, kv_rank)), _const_spec(w_uv.shape)],
        out_specs=_const_spec((n_tok, vw)),
        out_shape=jax.ShapeDtypeStruct((n_tok, vw), BF16),
        compiler_params=_params(("arbitrary",)), name="latent_out",
    )(o_lat.reshape(db * rows, kv_rank), w_uv)


def _gmlp_prompt(h, w, tm, chunk):
    n, d = h.shape
    width = w['w_in_c'].shape[1] // 2
    return pl.pallas_call(
        functools.partial(_gmlp_prompt_kernel, chunk=chunk, n_sub=max(1, tm // POST_GROUP_ROWS)),
        grid=(n // tm,),
        in_specs=[_row_spec(tm, d), _const_spec((1, d)), _const_spec(w['w_in_c'].shape),
                  _const_spec((1, width)), _const_spec(w['w_s'].shape),
                  _const_spec(w['bias_full'].shape)],
        out_specs=_row_spec(tm, width),
        out_shape=jax.ShapeDtypeStruct((n, width), BF16),
        compiler_params=_params(("arbitrary",)), name="gmlp_prompt",
    )(h, w['g_mix'], w['w_in_c'], w['g_v'], w['w_s'], w['bias_full'])


def _gmlp_sample(h, w, n_t):
    n, d = h.shape
    width = w['w_in_c'].shape[1] // 2
    return pl.pallas_call(
        functools.partial(_gmlp_sample_kernel, n_t=n_t),
        grid=(1,),
        in_specs=[_const_spec((n, d)), _const_spec((1, d)), _const_spec(w['w_in_c'].shape),
                  _const_spec((1, width)), _const_spec(w['ws_small'].shape),
                  _const_spec(w['bias_small'].shape)],
        out_specs=(_const_spec((n, width)), _const_spec((n, width))),
        out_shape=(jax.ShapeDtypeStruct((n, width), BF16), jax.ShapeDtypeStruct((n, width), F32)),
        compiler_params=_params(("arbitrary",)), name="gmlp_sample",
    )(h, w['g_mix'], w['w_in_c'], w['g_v'], w['ws_small'], w['bias_small'])


def _post(h, mixes, w_out, p, layer, w, tm, final, name):
    n, d = h.shape
    in_specs = ([_row_spec(tm, d)] + [_row_spec(tm, m.shape[1]) for m in mixes]
                + [_const_spec(w_out.shape),
                   pl.BlockSpec((None, tm, p.shape[2]), lambda i: (layer, i, 0)),
                   _const_spec((1, d)), _const_spec(w['w1'].shape),
                   _const_spec(w['w2'].shape), _const_spec((1, d)), _const_spec(w['w_gate'].shape),
                   _const_spec(w['w_ple'].shape), _const_spec((1, d))])
    return pl.pallas_call(
        functools.partial(_post_kernel, n_mix=len(mixes), ff_chunk=min(FF_CHUNK, w['w1'].shape[1]),
                          final=final, n_sub=max(1, tm // POST_GROUP_ROWS)),
        grid=(n // tm,), in_specs=in_specs, out_specs=_row_spec(tm, d),
        out_shape=jax.ShapeDtypeStruct((n, d), F32),
        compiler_params=_params(("arbitrary",)), name=name,
    )(h, *mixes, w_out, p, w['g_ffn'], w['w1'], w['w2'], w['g_ple'], w['w_gate'], w['w_ple'],
      w['g_f'])


def _rope_table(pos, rope_dim):
    inv = ROPE_THETA ** (-jnp.arange(0, rope_dim, 2, dtype=F32) / rope_dim)
    ang = pos.astype(F32)[:, None] * inv[None, :]
    c, s = jnp.cos(ang), jnp.sin(ang)
    t = pos.shape[0]
    cc = jnp.concatenate([c, c], axis=1)
    ss = jnp.concatenate([-s, s], axis=1)
    zeros = lambda w_: jnp.zeros((t, w_), F32)
    nope = LANES - 2 * rope_dim
    key_tab = jnp.concatenate([cc, ss, zeros(LANES - 2 * rope_dim)], axis=1)
    q_cos = jnp.concatenate([jnp.ones((t, nope), F32), cc, zeros(LANES - nope - rope_dim)], axis=1)
    q_sin = jnp.concatenate([zeros(nope), ss, zeros(LANES - nope - rope_dim)], axis=1)
    return jnp.concatenate([key_tab, q_cos, q_sin], axis=1)


def _prep_layer_ab(j, norm_mix_i, w_in_ab, conv_w, q_norm, w_uq, kv_norm, w_uk, w_uv):
    d = w_in_ab.shape[1]
    kv_rank, n_heads, nope = w_uk.shape[1:]
    rope = w_uq.shape[2] // n_heads - nope
    v_dim = w_uv.shape[3]
    half = rope // 2
    in_ab = w_in_ab.shape[2]
    w_in = w_in_ab[j]
    kpe0 = in_ab - rope
    pad = (-(in_ab + rope)) % (2 * LANES)
    w_in_p = jnp.concatenate(
        [w_in, w_in[:, kpe0 + half:], w_in[:, kpe0:kpe0 + half], jnp.zeros((d, pad), F32)], axis=1)
    q_rank = w_uq.shape[1]
    uq = w_uq[j].reshape(q_rank, n_heads, nope + rope)
    zpad = jnp.zeros((q_rank, n_heads, LANES - nope - rope), F32)
    wq_a = jnp.concatenate([uq, zpad], axis=2)
    wq_b = jnp.concatenate([jnp.zeros((q_rank, n_heads, nope), F32), uq[:, :, nope + half:],
                            uq[:, :, nope:nope + half], zpad], axis=2)
    w_q = jnp.concatenate([wq_a.reshape(q_rank, -1), wq_b.reshape(q_rank, -1)], axis=1)
    uk = w_uk[j]
    w_k = jnp.concatenate([uk, jnp.zeros((kv_rank, n_heads, LANES - nope), F32)], axis=2)
    w_ukt = jnp.concatenate([uk.transpose(1, 2, 0),
                             jnp.zeros((n_heads, LANES - nope, kv_rank), F32)], axis=1)
    return dict(
        g_mix=norm_mix_i.reshape(1, -1), w_in=w_in_p.astype(BF16), g_q=q_norm[j].reshape(1, -1),
        w_q=w_q.astype(BF16), w_qt=wq_a.reshape(q_rank, -1).T.astype(BF16),
        g_kv=kv_norm[j].reshape(1, -1), conv_w=conv_w[j],
        w_k=w_k.reshape(kv_rank, -1).astype(BF16),
        w_v=w_uv[j].reshape(kv_rank, n_heads * v_dim).astype(BF16),
        w_vt=w_uv[j].reshape(kv_rank, n_heads * v_dim).T.astype(BF16), w_ukt=w_ukt.astype(BF16),
        rope_dim=rope, n_heads=n_heads, nope=nope)


def _prep_layer_c(j, norm_mix_i, w_in_c, v_norm, w_s, b_s, n_t):
    n_groups, chunk = w_s.shape[1], w_s.shape[2]
    width = w_in_c.shape[2] // 2
    gw = width // n_groups
    bias_full = jnp.repeat(b_s[j].T, gw, axis=1)
    ws_small = jnp.repeat(w_s[j][:, :n_t, :n_t].transpose(1, 2, 0), gw, axis=2)
    return dict(g_mix=norm_mix_i.reshape(1, -1), w_in_c=w_in_c[j].astype(BF16),
                g_v=v_norm[j].reshape(1, -1), w_s=w_s[j], bias_full=bias_full,
                ws_small=ws_small, bias_small=bias_full[:n_t], chunk=chunk)


def _prep_post(i, norm_ffn, w_ff1, w_ff2, norm_ple, w_ple_gate, w_ple, norm_f):
    return dict(g_ffn=norm_ffn[i].reshape(1, -1), w1=w_ff1[i].astype(BF16), w2=w_ff2[i].astype(BF16),
                g_ple=norm_ple[i].reshape(1, -1), w_gate=w_ple_gate[i].astype(BF16),
                w_ple=w_ple[i].astype(BF16), g_f=norm_f.reshape(1, -1))


def _token_tile(n, cap):
    tm = min(cap, n)
    while n % tm:
        tm //= 2
    return tm


def kernel(x_prompt, x_sample, cache_ckv, cache_kpe, state_conv, page_table, p_prompt, p_sample, norm_mix, w_in_ab, conv_w, q_norm, w_uq, kv_norm, w_uk, w_uv, w_out_ab, w_in_c, v_norm, w_s, b_s, w_out_c, norm_ffn, w_ff1, w_ff2, norm_ple, w_ple_gate, w_ple, norm_f):
    depth = norm_mix.shape[0]
    batch, seq, d = x_prompt.shape
    db, n_t, _ = x_sample.shape
    n_pages = page_table.shape[1]
    page = cache_ckv.shape[2]
    past_len = n_pages * page
    n_heads, nope = w_uk.shape[2], w_uk.shape[3]
    rope = cache_kpe.shape[3]
    scale = float(nope + rope) ** -0.5

    tab_p = _rope_table(jnp.arange(seq, dtype=jnp.int32), rope)
    tab_s = jnp.repeat(_rope_table(past_len + jnp.arange(n_t, dtype=jnp.int32), rope), db, axis=0)

    n_p = batch * seq
    n_s = db * n_t
    tm_p = _token_tile(seq, 512)
    tm_post = _token_tile(seq, POST_TILE_ROWS)
    tq = _token_tile(seq, 512)
    hp = x_prompt.reshape(n_p, d)
    hs = x_sample.transpose(1, 0, 2).reshape(n_s, d)
    pp = p_prompt.reshape(depth, n_p, -1)
    ps = p_sample.transpose(0, 2, 1, 3).reshape(depth, n_s, -1)

    def unmajor(a):
        return a.reshape(n_t, db, a.shape[-1]).transpose(1, 0, 2)

    conv_p, conv_s, ckv_p, kpe_p, ckv_s, kpe_s, v_s = [], [], [], [], [], [], []
    for i in range(depth):
        j = i // 2
        wpost = _prep_post(i, norm_ffn, w_ff1, w_ff2, norm_ple, w_ple_gate, w_ple, norm_f)
        final = i == depth - 1
        if i % 2 == 0:
            w = _prep_layer_ab(j, norm_mix[i], w_in_ab, conv_w, q_norm, w_uq, kv_norm, w_uk, w_uv)
            w_out = w_out_ab[j].astype(BF16)
            hr = rope // 2
            cos_sin_t = jnp.concatenate([tab_p[:, :hr], tab_p[:, rope + hr:2 * rope]], axis=1).T
            yconv, ckv, kpe, qt, k, vt, cstate = _ab_prompt(
                hp, tab_p[:, :LANES], cos_sin_t, w, batch, seq, tm_p)
            yatt_p = _flash(qt, k.reshape(batch, seq, -1), vt, scale, n_heads, tq)
            conv_p.append(cstate[:, SUBLANES - 2:, :])
            ckv_p.append(ckv.reshape(batch, seq, -1))
            kpe_p.append(kpe.reshape(batch, seq, -1))
            state = state_conv[j].transpose(1, 0, 2)
            yconv_s, ckv, kpe, q, qlat, up = _ab_sample(hs, tab_s, w, state, n_t)
            kv_rank = ckv.shape[1]
            ql = qlat.reshape(n_heads, n_t, db, kv_rank).transpose(2, 1, 0, 3).reshape(
                db, n_t * n_heads, kv_rank)
            qp = q.reshape(n_t, db, n_heads, LANES)[..., nope:nope + rope].transpose(
                1, 0, 2, 3).reshape(db, n_t * n_heads, rope)
            ckv_b = unmajor(ckv)
            kpe_b = unmajor(kpe)
            padr = ((0, 0), (0, SUBLANES - n_t), (0, 0))
            hp = _post(hp, [yconv, yatt_p.reshape(n_p, -1)], w_out, pp, i, wpost, tm_post, final,
                       "post_prompt")
            yatt = _paged(page_table, ql, qp, jnp.pad(ckv_b, padr), jnp.pad(kpe_b, padr),
                          w['w_v'], cache_ckv[j], jnp.swapaxes(cache_kpe[j], 1, 2), scale, n_heads)
            yatt = yatt.reshape(db, n_t, -1).transpose(1, 0, 2).reshape(n_s, -1)
            hs = _post(hs, [yconv_s, yatt], w_out, ps, i, wpost, n_s, final, "post_sample")
            conv_s.append(unmajor(up)[:, n_t - 2:, :])
            ckv_s.append(ckv_b)
            kpe_s.append(kpe_b)
        else:
            w = _prep_layer_c(j, norm_mix[i], w_in_c, v_norm, w_s, b_s, n_t)
            w_out = w_out_c[j].astype(BF16)
            m = _gmlp_prompt(hp, w, tm_post, w['chunk'])
            hp = _post(hp, [m], w_out, pp, i, wpost, tm_post, final, "post_prompt")
            m, v = _gmlp_sample(hs, w, n_t)
            hs = _post(hs, [m], w_out, ps, i, wpost, n_s, final, "post_sample")
            v_s.append(unmajor(v))

    y_prompt = hp.reshape(batch, seq, d)
    y_sample = unmajor(hs)
    return (y_prompt, y_sample, jnp.stack(conv_p), jnp.stack(conv_s), jnp.stack(ckv_p),
            jnp.stack(kpe_p), jnp.stack(ckv_s), jnp.stack(kpe_s), jnp.stack(v_s))
```
